```python
import math
import jax
import jax.numpy as jnp
from jax import lax
import numpy as np


D_MODEL = 2048
BATCH = 2
SEQ = 8192
DEPTH = 4
DEC_BATCH = 2
DEC_SEQ = 16384
PAST_LEN = 128

HEAD_DIM = 128
N_MIXERS = 4
HEADS_PER_MIXER = D_MODEL // HEAD_DIM // N_MIXERS
GROUP_WIDTH = HEADS_PER_MIXER * HEAD_DIM
MIX_WIDTH = N_MIXERS * GROUP_WIDTH
IN_WIDTH = 15 * GROUP_WIDTH + 4 * HEADS_PER_MIXER
D_FF = 4 * D_MODEL
NORM_EPS = 1e-6
ROPE_THETA = 500000.0
ROPE_DIM = HEAD_DIM // 4
DILATED_CONFIGS = ((128, 1), (512, 4), (2048, 16))
GRID_W = 64
NA_KH = 8
NA_KW = 16
HGRN_CHUNK = 64
DN_CHUNK = 64
DN_CONV = 5
MASK_VALUE = -1e30
LOG_FLOOR = 1e-30

kernel_name = 'hybrid_bidir_encoder_parallel_heads'

F32 = jnp.float32


def rms_norm(x, g):
    xf = x.astype(F32)
    y = xf * lax.rsqrt(jnp.mean(xf * xf, axis=-1, keepdims=True) + NORM_EPS)
    return (y * g.astype(F32)).astype(x.dtype)


def l2_normalize(x):
    xf = x.astype(F32)
    return xf * lax.rsqrt(jnp.sum(xf * xf, axis=-1, keepdims=True) + NORM_EPS)


def masked_exp(mask, logits):
    return jnp.where(mask, jnp.exp(jnp.where(mask, logits, 0.0)), 0.0)


def partial_rope(x, positions):
    half = ROPE_DIM // 2
    inv_freq = jnp.power(ROPE_THETA, -jnp.arange(half, dtype=F32) / half)
    ang = positions.astype(F32)[:, None] * inv_freq[None, :]
    cos = jnp.cos(ang)[None, :, None, :]
    sin = jnp.sin(ang)[None, :, None, :]
    xr = x[..., :ROPE_DIM].astype(F32)
    x1, x2 = xr[..., :half], xr[..., half:]
    rot = jnp.concatenate([x1 * cos - x2 * sin, x2 * cos + x1 * sin], axis=-1)
    return jnp.concatenate([rot.astype(x.dtype), x[..., ROPE_DIM:]], axis=-1)


def banded_window_attention(q, k, v, radius):
    lead, n, dh = q.shape[:-2], q.shape[-2], q.shape[-1]
    nd = len(lead)
    blk = radius
    nb = -(-n // blk)
    extra = nb * blk - n
    qb = jnp.pad(q, [(0, 0)] * nd + [(0, extra), (0, 0)]).reshape(*lead, nb, blk, dh)

    def key_blocks(t):
        tp = jnp.pad(t, [(0, 0)] * nd + [(blk, extra + blk), (0, 0)]).reshape(*lead, nb + 2, blk, dh)
        return jnp.concatenate([tp[..., j:j + nb, :, :] for j in range(3)], axis=-2)

    kb, vb = key_blocks(k), key_blocks(v)
    s = jnp.einsum('...qd,...kd->...qk', qb, kb, preferred_element_type=F32) * (dh ** -0.5)
    qi = (jnp.arange(nb) * blk)[:, None] + jnp.arange(blk)[None, :]
    ki = (jnp.arange(nb) * blk - blk)[:, None] + jnp.arange(3 * blk)[None, :]
    valid = ((jnp.abs(qi[:, :, None] - ki[:, None, :]) <= radius)
             & (ki[:, None, :] >= 0) & (ki[:, None, :] < n))
    s = jnp.where(valid, s, MASK_VALUE)
    m = jnp.max(s, axis=-1, keepdims=True)
    e = masked_exp(valid, s - m)
    den = jnp.sum(e, axis=-1, keepdims=True)
    o = jnp.einsum('...qk,...kd->...qd', (e / den).astype(v.dtype), vb)
    lse = (m + jnp.log(den))[..., 0]
    o = o.reshape(*lead, nb * blk, dh)[..., :n, :]
    lse = lse.reshape(*lead, nb * blk)[..., :n]
    return o, lse


def dilated_attention(q, k, v):
    B, L, H, dh = q.shape
    outs, lses = [], []
    for window, dil in DILATED_CONFIGS:
        radius = window // (2 * dil)

        def to_strided(t):
            return t.reshape(B, L // dil, dil, H, dh).transpose(0, 3, 2, 1, 4)

        o, lse = banded_window_attention(to_strided(q), to_strided(k), to_strided(v), radius)
        outs.append(o.transpose(0, 3, 2, 1, 4).reshape(B, L, H, dh).astype(F32))
        lses.append(lse.transpose(0, 3, 2, 1).reshape(B, L, H))
    w = jax.nn.softmax(jnp.stack(lses, axis=-1), axis=-1)
    return jnp.einsum('blhn,nblhd->blhd', w, jnp.stack(outs, axis=0))


def swa_mixer(q, k, v, q_norm, k_norm):
    B, L, _ = q.shape
    shp = (B, L, HEADS_PER_MIXER, HEAD_DIM)
    pos = jnp.arange(L)
    qh = partial_rope(rms_norm(q.reshape(shp), q_norm), pos)
    kh = partial_rope(rms_norm(k.reshape(shp), k_norm), pos)
    o = dilated_attention(qh, kh, v.reshape(shp))
    return o.reshape(B, L, GROUP_WIDTH)


def gla_chunk_scan(q, k, v, log_f):
    B, H, L, dk = q.shape
    dv = v.shape[-1]
    C = HGRN_CHUNK
    n = L // C

    def chunks(t):
        return jnp.moveaxis(t.reshape(B, H, n, C, t.shape[-1]), 2, 0)

    incl = jnp.tril(jnp.ones((C, C), bool))[:, :, None]

    def step(S, inp):
        q_c, k_c, v_c, g_c = inp
        b = jnp.cumsum(g_c, axis=-2)
        decay = masked_exp(incl, b[..., :, None, :] - b[..., None, :, :])
        a = jnp.einsum('bhik,bhjk,bhijk->bhij', q_c, k_c, decay)
        o = (jnp.einsum('bhik,bhkv->bhiv', q_c * jnp.exp(b), S)
             + jnp.einsum('bhij,bhjv->bhiv', a, v_c))
        b_last = b[..., -1:, :]
        S = (jnp.exp(b_last[..., 0, :])[..., None] * S
             + jnp.einsum('bhjk,bhjv->bhkv', k_c * jnp.exp(b_last - b), v_c))
        return S, o

    S0 = jnp.zeros((B, H, dk, dv), F32)
    _, o = lax.scan(step, S0, (chunks(q), chunks(k), chunks(v), chunks(log_f)))
    return jnp.moveaxis(o, 0, 2).reshape(B, H, L, dv)


def hgrn2_mixer(q, f_fwd, f_bwd, i, g, lb, norm_g):
    B, L, _ = q.shape
    H = HEADS_PER_MIXER

    def heads(t):
        return t.astype(F32).reshape(B, L, H, HEAD_DIM).transpose(0, 2, 1, 3)

    def flip(t):
        return jnp.flip(t, axis=2)

    qh = heads(jax.nn.silu(q))
    vh = heads(i)
    gates = []
    for d, f_in in enumerate((f_fwd, f_bwd)):
        lbd = lb[d]
        xf = f_in.astype(F32)
        f = lbd + (1.0 - lbd) * jax.nn.sigmoid(xf)
        one_minus_f = (1.0 - lbd) * jax.nn.sigmoid(-xf)
        log_f = jnp.log(jnp.maximum(f, LOG_FLOOR))
        gates.append((heads(one_minus_f), heads(log_f)))
    (k_f, g_f), (k_b, g_b) = gates
    o_fwd = gla_chunk_scan(qh, k_f, vh, g_f)
    o_bwd = flip(gla_chunk_scan(flip(qh), flip(k_b), flip(vh), flip(g_b)))
    o = (o_fwd + o_bwd).transpose(0, 2, 1, 3)
    o = rms_norm(o, norm_g) * jax.nn.silu(g.astype(F32).reshape(B, L, H, HEAD_DIM))
    return o.reshape(B, L, GROUP_WIDTH)


def neighbourhood_attention(q, k, v, rpb):
    B, L, H, dh = q.shape
    rows = L // GRID_W
    kh = min(NA_KH, rows)
    kw = NA_KW
    r = jnp.arange(rows)
    r0 = jnp.clip(r - kh // 2, 0, rows - kh)
    key_rows = r0[:, None] + jnp.arange(kh)[None, :]
    cq = jnp.arange(GRID_W)
    c0 = jnp.clip(cq - kw // 2, 0, GRID_W - kw)
    ck = jnp.arange(GRID_W)
    col_ok = (ck[None, :] >= c0[:, None]) & (ck[None, :] < c0[:, None] + kw)
    qg = q.reshape(B, rows, GRID_W, H, dh)
    kg = k.reshape(B, rows, GRID_W, H, dh)[:, key_rows]
    vg = v.reshape(B, rows, GRID_W, H, dh)[:, key_rows]
    s = jnp.einsum('brqhd,brawhd->bhrqaw', qg, kg, preferred_element_type=F32) * (dh ** -0.5)
    row_off = key_rows - r[:, None] + (NA_KH - 1)
    col_off = jnp.clip(ck[None, :] - cq[:, None], -(kw - 1), kw - 1) + (NA_KW - 1)
    bias = rpb.astype(F32)[:, row_off]
    bias = bias[..., col_off].transpose(0, 1, 3, 2, 4)
    s = s + bias[None]
    s = jnp.where(col_ok[None, None, None, :, None, :], s, MASK_VALUE)
    p = jax.nn.softmax(s.reshape(B, H, rows, GRID_W, kh * GRID_W), axis=-1)
    p = p.reshape(B, H, rows, GRID_W, kh, GRID_W).astype(v.dtype)
    o = jnp.einsum('bhrqaw,brawhd->brqhd', p, vg)
    return o.reshape(B, L, H, dh)


def na_mixer(q, k, v, q_norm, k_norm, rpb):
    B, L, _ = q.shape
    shp = (B, L, HEADS_PER_MIXER, HEAD_DIM)
    o = neighbourhood_attention(rms_norm(q.reshape(shp), q_norm), rms_norm(k.reshape(shp), k_norm),
                                v.reshape(shp), rpb)
    return o.reshape(B, L, GROUP_WIDTH)


def centred_depthwise_conv(x, w):
    K = w.shape[0]
    return lax.conv_general_dilated(x, w[:, None, :].astype(x.dtype), window_strides=(1,),
                                    padding=[(K // 2, K // 2)], dimension_numbers=('NWC', 'WIO', 'NWC'),
                                    feature_group_count=x.shape[-1])


def gated_delta_chunk_scan(q, k, v, g, beta):
    B, H, L, dk = q.shape
    dv = v.shape[-1]
    C = DN_CHUNK
    n = L // C
    q, k, v = (t.reshape(B, H, n, C, t.shape[-1]) for t in (q, k, v))
    g = g.reshape(B, H, n, C)
    beta = beta.reshape(B, H, n, C)
    G = jnp.cumsum(g, axis=-1)
    incl = jnp.tril(jnp.ones((C, C), bool))
    strict = jnp.tril(jnp.ones((C, C), bool), -1)
    gamma = masked_exp(incl, G[..., :, None] - G[..., None, :])
    kb = k * beta[..., None]
    n_mat = jnp.where(strict, jnp.einsum('bhnid,bhnjd->bhnij', kb, k) * gamma, 0.0)
    t_mat = n_mat + jnp.eye(C, dtype=F32)
    rhs = jnp.concatenate([kb * jnp.exp(G)[..., None], v * beta[..., None]], axis=-1)
    sol = lax.linalg.triangular_solve(t_mat, rhs, left_side=True, lower=True, unit_diagonal=True)
    w_c, u_c = sol[..., :dk], sol[..., dk:]
    a_qk = jnp.einsum('bhnid,bhnjd->bhnij', q, k) * gamma
    q_dec = q * jnp.exp(G)[..., None]
    k_dec = k * jnp.exp(G[..., -1:] - G)[..., None]
    last = jnp.exp(G[..., -1])

    def step(S, inp):
        w_i, u_i, a_i, q_i, k_i, l_i = inp
        v_new = u_i - jnp.einsum('bhik,bhkv->bhiv', w_i, S)
        o = jnp.einsum('bhik,bhkv->bhiv', q_i, S) + jnp.einsum('bhij,bhjv->bhiv', a_i, v_new)
        S = l_i[..., None, None] * S + jnp.einsum('bhjk,bhjv->bhkv', k_i, v_new)
        return S, o

    def mv(t):
        return jnp.moveaxis(t, 2, 0)

    S0 = jnp.zeros((B, H, dk, dv), F32)
    _, o = lax.scan(step, S0, (mv(w_c), mv(u_c), mv(a_qk), mv(q_dec), mv(k_dec), mv(last)))
    return jnp.moveaxis(o, 0, 2).reshape(B, H, L, dv)


def gated_deltanet_mixer(qkv, z, a, b, conv_w, a_log, dt_bias, norm_g):
    B, L, _ = qkv.shape
    H = HEADS_PER_MIXER
    qkv = jax.nn.silu(centred_depthwise_conv(qkv, conv_w))
    q, k, v = jnp.split(qkv, 3, axis=-1)

    def heads(t):
        return t.astype(F32).reshape(B, L, H, HEAD_DIM).transpose(0, 2, 1, 3)

    def flip(t):
        return jnp.flip(t, axis=2)

    qh = l2_normalize(heads(q)) * (HEAD_DIM ** -0.5)
    kh = l2_normalize(heads(k))
    vh = heads(v)
    a = a.astype(F32).reshape(B, L, 2, H)
    b = b.astype(F32).reshape(B, L, 2, H)
    g = (-jnp.exp(a_log.astype(F32)) * jax.nn.softplus(a + dt_bias.astype(F32))).transpose(2, 0, 3, 1)
    beta = jax.nn.sigmoid(b).transpose(2, 0, 3, 1)
    o_fwd = gated_delta_chunk_scan(qh, kh, vh, g[0], beta[0])
    o_bwd = flip(gated_delta_chunk_scan(flip(qh), flip(kh), flip(vh), flip(g[1]), flip(beta[1])))
    o = (o_fwd + o_bwd).transpose(0, 2, 1, 3)
    o = rms_norm(o, norm_g) * jax.nn.silu(z.astype(F32).reshape(B, L, H, HEAD_DIM))
    return o.reshape(B, L, GROUP_WIDTH)


def encoder_layer(x, c, norm1_g, norm2_g, ada_w, ada_b, w_in, w_out, swa_q_norm, swa_k_norm,
                  hgrn_lb, hgrn_norm_g, na_q_norm, na_k_norm, na_rpb, dn_conv_w, dn_a_log,
                  dn_dt_bias, dn_norm_g, w_mlp_in, w_mlp_out):
    mod = jnp.einsum('bd,de->be', jax.nn.silu(c), ada_w) + ada_b
    shift1, scale1, gate1, shift2, scale2, gate2 = (t[:, None, :] for t in jnp.split(mod, 6, axis=-1))
    h = rms_norm(x, norm1_g) * (1 + scale1) + shift1
    u = jnp.einsum('bld,de->ble', h, w_in)
    G, H2 = GROUP_WIDTH, 2 * HEADS_PER_MIXER
    sizes = (G, G, G, G, G, G, G, G, G, G, G, 3 * G, G, H2, H2)
    parts, start = [], 0
    for s in sizes:
        parts.append(u[..., start:start + s])
        start += s
    (swa_q, swa_k, swa_v, hg_q, hg_ff, hg_fb, hg_i, hg_g,
     na_q, na_k, na_v, dn_qkv, dn_z, dn_a, dn_b) = parts
    y = jnp.concatenate([
        swa_mixer(swa_q, swa_k, swa_v, swa_q_norm, swa_k_norm),
        hgrn2_mixer(hg_q, hg_ff, hg_fb, hg_i, hg_g, hgrn_lb, hgrn_norm_g),
        na_mixer(na_q, na_k, na_v, na_q_norm, na_k_norm, na_rpb),
        gated_deltanet_mixer(dn_qkv, dn_z, dn_a, dn_b, dn_conv_w, dn_a_log, dn_dt_bias, dn_norm_g),
    ], axis=-1).astype(x.dtype)
    x = x + gate1 * jnp.einsum('ble,ed->bld', y, w_out)
    h = rms_norm(x, norm2_g) * (1 + scale2) + shift2
    hid = jnp.square(jax.nn.relu(jnp.einsum('bld,df->blf', h, w_mlp_in)))
    return x + gate2 * jnp.einsum('blf,fd->bld', hid, w_mlp_out)


def run_trunk(x, c, norm1_g, norm2_g, ada_w, ada_b, w_in, w_out, swa_q_norm, swa_k_norm,
              hgrn_lb_logits, hgrn_norm_g, na_q_norm, na_k_norm, na_rpb, dn_conv_w, dn_a_log,
              dn_dt_bias, dn_norm_g, w_mlp_in, w_mlp_out):
    p = jax.nn.softmax(hgrn_lb_logits.astype(F32), axis=0)
    lower_bounds = jnp.cumsum(p, axis=0) - p[0:1]
    for l in range(DEPTH):
        x = encoder_layer(x, c, norm1_g[l], norm2_g[l], ada_w[l], ada_b[l], w_in[l], w_out[l],
                          swa_q_norm[l], swa_k_norm[l], lower_bounds[l], hgrn_norm_g[l],
                          na_q_norm[l], na_k_norm[l], na_rpb[l], dn_conv_w[l], dn_a_log[l],
                          dn_dt_bias[l], dn_norm_g[l], w_mlp_in[l], w_mlp_out[l])
    return x


def setup_inputs(seed: int = 0) -> dict:
    key = jax.random.key(seed)
    ks = jax.random.split(key, 26)
    D = D_MODEL
    H = HEADS_PER_MIXER

    def nrm(k, shape, scale):
        return jax.random.normal(k, shape, F32) * scale

    dt = jnp.exp(jax.random.uniform(ks[19], (DEPTH, 2, H), F32, math.log(1e-3), math.log(1e-1)))
    return {
        'x_prompt': nrm(ks[0], (BATCH, SEQ, D), 1.0),
        'x_sample': nrm(ks[1], (DEC_BATCH, DEC_SEQ, D), 1.0),
        'c_prompt': nrm(ks[2], (BATCH, D), 1.0),
        'c_sample': nrm(ks[3], (DEC_BATCH, D), 1.0),
        'norm1_g': 1.0 + nrm(ks[4], (DEPTH, D), 0.02),
        'norm2_g': 1.0 + nrm(ks[5], (DEPTH, D), 0.02),
        'ada_w': nrm(ks[6], (DEPTH, D, 6 * D), 0.5 * D ** -0.5),
        'ada_b': nrm(ks[7], (DEPTH, 6 * D), 0.02),
        'w_in': nrm(ks[8], (DEPTH, D, IN_WIDTH), D ** -0.5),
        'w_out': nrm(ks[9], (DEPTH, MIX_WIDTH, D), MIX_WIDTH ** -0.5),
        'swa_q_norm': 1.0 + nrm(ks[10], (DEPTH, HEAD_DIM), 0.02),
        'swa_k_norm': 1.0 + nrm(ks[11], (DEPTH, HEAD_DIM), 0.02),
        'hgrn_lb_logits': nrm(ks[12], (DEPTH, 2, GROUP_WIDTH), 1.0),
        'hgrn_norm_g': 1.0 + nrm(ks[13], (DEPTH, HEAD_DIM), 0.02),
        'na_q_norm': 1.0 + nrm(ks[14], (DEPTH, HEAD_DIM), 0.02),
        'na_k_norm': 1.0 + nrm(ks[15], (DEPTH, HEAD_DIM), 0.02),
        'na_rpb': nrm(ks[16], (DEPTH, H, 2 * NA_KH - 1, 2 * NA_KW - 1), 0.5),
        'dn_conv_w': nrm(ks[17], (DEPTH, DN_CONV, 3 * GROUP_WIDTH), DN_CONV ** -0.5),
        'dn_a_log': jnp.log(jax.random.uniform(ks[18], (DEPTH, 2, H), F32, 1.0, 16.0)),
        'dn_dt_bias': dt + jnp.log(-jnp.expm1(-dt)),
        'dn_norm_g': 1.0 + nrm(ks[20], (DEPTH, HEAD_DIM), 0.02),
        'w_mlp_in': nrm(ks[21], (DEPTH, D, D_FF), D ** -0.5),
        'w_mlp_out': nrm(ks[22], (DEPTH, D_FF, D), D_FF ** -0.5),
    }


def reference(x_prompt, x_sample, c_prompt, c_sample, norm1_g, norm2_g, ada_w, ada_b, w_in, w_out,
              swa_q_norm, swa_k_norm, hgrn_lb_logits, hgrn_norm_g, na_q_norm, na_k_norm, na_rpb,
              dn_conv_w, dn_a_log, dn_dt_bias, dn_norm_g, w_mlp_in, w_mlp_out):
    y_prompt = run_trunk(x_prompt, c_prompt, norm1_g, norm2_g, ada_w, ada_b, w_in, w_out,
                         swa_q_norm, swa_k_norm, hgrn_lb_logits, hgrn_norm_g, na_q_norm, na_k_norm,
                         na_rpb, dn_conv_w, dn_a_log, dn_dt_bias, dn_norm_g, w_mlp_in, w_mlp_out)
    y_sample = run_trunk(x_sample, c_sample, norm1_g, norm2_g, ada_w, ada_b, w_in, w_out,
                         swa_q_norm, swa_k_norm, hgrn_lb_logits, hgrn_norm_g, na_q_norm, na_k_norm,
                         na_rpb, dn_conv_w, dn_a_log, dn_dt_bias, dn_norm_g, w_mlp_in, w_mlp_out)
    return (y_prompt, y_sample)
```

```python
import functools
import math

import jax
import jax.numpy as jnp
from jax import lax
from jax.experimental import pallas as pl
from jax.experimental.pallas import tpu as pltpu

F32 = jnp.float32
BF16 = jnp.bfloat16

D_MODEL = 2048
DEPTH = 4
HEAD_DIM = 128
N_HEADS = 4
GROUP_WIDTH = N_HEADS * HEAD_DIM
IN_MAIN = 15 * GROUP_WIDTH
D_FF = 4 * D_MODEL
NORM_EPS = 1e-6
ROPE_THETA = 500000.0
ROPE_DIM = HEAD_DIM // 4
DILATED_CONFIGS = ((128, 1), (512, 4), (2048, 16))
GRID_W = 64
NA_KH = 8
NA_KW = 16
CHUNK = 64
SUB = 16
DN_CONV = 5
MASK_VALUE = -1e30
LOG_FLOOR = 1e-30
ATTN_SCALE = HEAD_DIM ** -0.5
EXP_CLAMP = 60.0
HIGHEST = lax.Precision.HIGHEST

P_SWA_Q, P_SWA_K, P_SWA_V = 0, 1, 2
P_HG_Q, P_HG_FF, P_HG_FB, P_HG_I, P_HG_G = 3, 4, 5, 6, 7
P_NA_Q, P_NA_K, P_NA_V = 8, 9, 10
P_DN_QKV, P_DN_Z = 11, 14

VMEM_LIMIT = 56 * 1024 * 1024


def _cparams(*sem):
    return pltpu.CompilerParams(dimension_semantics=sem, vmem_limit_bytes=VMEM_LIMIT)


def _dot(a, b):
    return jnp.dot(a, b, preferred_element_type=F32)


def _dot_nt(a, b):
    return lax.dot_general(a, b, (((1,), (1,)), ((), ())), preferred_element_type=F32)


def _dot_tn(a, b):
    return lax.dot_general(a, b, (((0,), (0,)), ((), ())), preferred_element_type=F32)


def _dot_hi(a, b):
    return jnp.dot(a, b, preferred_element_type=F32, precision=HIGHEST)


def _sigmoid(x):
    return jax.nn.sigmoid(x)


def _silu(x):
    return x * _sigmoid(x)


def _rms(x):
    return x * lax.rsqrt(jnp.mean(x * x, axis=-1, keepdims=True) + NORM_EPS)


def _cumsum_rows(tri_bf16, x):
    hi = x.astype(BF16)
    r1 = x - hi.astype(F32)
    mid = r1.astype(BF16)
    lo = (r1 - mid.astype(F32)).astype(BF16)
    return _dot(tri_bf16, hi) + _dot(tri_bf16, mid) + _dot(tri_bf16, lo)


def _mod_kernel(c_ref, w_ref, b_ref, o_ref):
    a = _silu(c_ref[...]).astype(BF16)
    o_ref[0] = _dot(a, w_ref[0].astype(BF16)) + b_ref[0]


def _modulation(c_all, ada_w, ada_b):
    rows, d = c_all.shape
    n = ada_w.shape[-1]
    tn = 1024
    return pl.pallas_call(
        _mod_kernel,
        grid=(DEPTH, n // tn),
        in_specs=[
            pl.BlockSpec((rows, d), lambda l, j: (0, 0)),
            pl.BlockSpec((1, d, tn), lambda l, j: (l, 0, j)),
            pl.BlockSpec((1, 1, tn), lambda l, j: (l, 0, j)),
        ],
        out_specs=pl.BlockSpec((1, rows, tn), lambda l, j: (l, 0, j)),
        out_shape=jax.ShapeDtypeStruct((DEPTH, rows, n), F32),
        compiler_params=_cparams("parallel", "parallel"),
        name="adaln_mod",
    )(c_all, ada_w, ada_b.reshape(DEPTH, 1, n))


NORM_ROWS = 256


def _norm_mod_store(h_ref, x_ref, g, scale, shift, tm):
    def body(c, carry):
        rows = pl.ds(pl.multiple_of(c * NORM_ROWS, NORM_ROWS), NORM_ROWS)
        y = _rms(x_ref[0, rows, :]) * g
        h_ref[rows, :] = (y * (1.0 + scale) + shift).astype(BF16)
        return carry
    lax.fori_loop(0, tm // NORM_ROWS, body, 0)


def _in_proj_kernel(x_ref, g_ref, mod_ref, w_ref, wab_ref, u_ref, uab_ref, h_ref, *, tm):
    @pl.when(pl.program_id(2) == 0)
    def _():
        _norm_mod_store(h_ref, x_ref, g_ref[...], mod_ref[0, 1:2, :], mod_ref[0, 0:1, :], tm)
        uab_ref[0] = _dot(h_ref[...], wab_ref[...])
    u_ref[0] = _dot(h_ref[...], w_ref[...])


def _in_proj(x, g, mod, w_main, w_ab, tm=512, tn=1536):
    B, L, D = x.shape
    N = w_main.shape[1]
    return pl.pallas_call(
        functools.partial(_in_proj_kernel, tm=tm),
        grid=(B, L // tm, N // tn),
        in_specs=[
            pl.BlockSpec((1, tm, D), lambda b, i, j: (b, i, 0)),
            pl.BlockSpec((1, D), lambda b, i, j: (0, 0)),
            pl.BlockSpec((1, 6, D), lambda b, i, j: (b, 0, 0)),
            pl.BlockSpec((D, tn), lambda b, i, j: (0, j)),
            pl.BlockSpec((D, HEAD_DIM), lambda b, i, j: (0, 0)),
        ],
        out_specs=[
            pl.BlockSpec((1, tm, tn), lambda b, i, j: (b, i, j)),
            pl.BlockSpec((1, tm, HEAD_DIM), lambda b, i, j: (b, i, 0)),
        ],
        out_shape=[jax.ShapeDtypeStruct((B, L, N), F32),
                   jax.ShapeDtypeStruct((B, L, HEAD_DIM), F32)],
        scratch_shapes=[pltpu.VMEM((tm, D), BF16)],
        compiler_params=_cparams("parallel", "parallel", "arbitrary"),
        name="in_proj",
    )(x, g.reshape(1, D), mod, w_main, w_ab)


def _out_proj_kernel(x_ref, mod_ref, y0_ref, y1_ref, y2_ref, y3_ref, w_ref, o_ref):
    acc = None
    for m, y_ref in enumerate((y0_ref, y1_ref, y2_ref, y3_ref)):
        part = _dot(y_ref[0], w_ref[m * GROUP_WIDTH:(m + 1) * GROUP_WIDTH, :])
        acc = part if acc is None else acc + part
    o_ref[0] = x_ref[0] + mod_ref[0, 2:3, :] * acc


def _out_proj(x, mod, ys, w_out, tm=512):
    B, L, D = x.shape
    yspec = pl.BlockSpec((1, tm, GROUP_WIDTH), lambda b, i: (b, i, 0))
    return pl.pallas_call(
        _out_proj_kernel,
        grid=(B, L // tm),
        in_specs=[
            pl.BlockSpec((1, tm, D), lambda b, i: (b, i, 0)),
            pl.BlockSpec((1, 6, D), lambda b, i: (b, 0, 0)),
            yspec, yspec, yspec, yspec,
            pl.BlockSpec((D, D), lambda b, i: (0, 0)),
        ],
        out_specs=pl.BlockSpec((1, tm, D), lambda b, i: (b, i, 0)),
        out_shape=jax.ShapeDtypeStruct((B, L, D), F32),
        compiler_params=_cparams("parallel", "parallel"),
        name="out_proj",
    )(x, mod, *ys, w_out)


def _mlp_kernel(x_ref, g_ref, mod_ref, w1_ref, w2_ref, o_ref, h_ref, *, tm):
    j = pl.program_id(2)

    @pl.when(j == 0)
    def _():
        _norm_mod_store(h_ref, x_ref, g_ref[...], mod_ref[0, 4:5, :], mod_ref[0, 3:4, :], tm)

    hid = jnp.square(jnp.maximum(_dot(h_ref[...], w1_ref[...]), 0.0)).astype(BF16)
    part = _dot(hid, w2_ref[...])

    @pl.when(j == 0)
    def _():
        o_ref[0] = part

    @pl.when(j > 0)
    def _():
        o_ref[0] += part

    @pl.when(j == pl.num_programs(2) - 1)
    def _():
        o_ref[0] = x_ref[0] + mod_ref[0, 5:6, :] * o_ref[0]


def _mlp(x, g, mod, w1, w2, tm=512, tf=1024):
    B, L, D = x.shape
    F = w1.shape[1]
    return pl.pallas_call(
        functools.partial(_mlp_kernel, tm=tm),
        grid=(B, L // tm, F // tf),
        in_specs=[
            pl.BlockSpec((1, tm, D), lambda b, i, j: (b, i, 0)),
            pl.BlockSpec((1, D), lambda b, i, j: (0, 0)),
            pl.BlockSpec((1, 6, D), lambda b, i, j: (b, 0, 0)),
            pl.BlockSpec((D, tf), lambda b, i, j: (0, j)),
            pl.BlockSpec((tf, D), lambda b, i, j: (j, 0)),
        ],
        out_specs=pl.BlockSpec((1, tm, D), lambda b, i, j: (b, i, 0)),
        out_shape=jax.ShapeDtypeStruct((B, L, D), F32),
        scratch_shapes=[pltpu.VMEM((tm, D), BF16)],
        compiler_params=_cparams("parallel", "parallel", "arbitrary"),
        name="mlp",
    )(x, g.reshape(1, D), mod, w1, w2)


def _qk_prep_kernel(*refs, rope):
    if rope:
        q_ref, k_ref, gq_ref, gk_ref, c_ref, s1_ref, s2_ref, qo_ref, ko_ref = refs
    else:
        q_ref, k_ref, gq_ref, gk_ref, qo_ref, ko_ref = refs
    for src, g_ref, dst in ((q_ref, gq_ref, qo_ref), (k_ref, gk_ref, ko_ref)):
        for h in range(N_HEADS):
            cols = slice(h * HEAD_DIM, (h + 1) * HEAD_DIM)
            y = _rms(src[0, :, cols]) * g_ref[...]
            if rope:
                y = (y * c_ref[...] + pltpu.roll(y, ROPE_DIM // 2, 1) * s1_ref[...]
                     + pltpu.roll(y, HEAD_DIM - ROPE_DIM // 2, 1) * s2_ref[...])
            dst[0, :, cols] = y.astype(dst.dtype)


def _rope_tables(L):
    half = ROPE_DIM // 2
    inv_freq = jnp.power(ROPE_THETA, -jnp.arange(half, dtype=F32) / half)
    ang = jnp.arange(L).astype(F32)[:, None] * inv_freq[None, :]
    cos, sin = jnp.cos(ang), jnp.sin(ang)
    zeros = jnp.zeros((L, HEAD_DIM - ROPE_DIM), F32)
    zh = jnp.zeros((L, half), F32)
    c = jnp.concatenate([cos, cos, jnp.ones((L, HEAD_DIM - ROPE_DIM), F32)], axis=1)
    s1 = jnp.concatenate([zh, sin, zeros], axis=1)
    s2 = jnp.concatenate([-sin, zh, zeros], axis=1)
    return c, s1, s2


def _qk_prep(u, part_q, part_k, gq, gk, rope_tabs, out_dtype, tl=512):
    B, L, _ = u.shape
    rope = rope_tabs is not None
    uspec = lambda part: pl.BlockSpec((1, tl, GROUP_WIDTH), lambda b, t, part=part: (b, t, part))
    gspec = pl.BlockSpec((1, HEAD_DIM), lambda b, t: (0, 0))
    tspec = pl.BlockSpec((tl, HEAD_DIM), lambda b, t: (t, 0))
    ospec = pl.BlockSpec((1, tl, GROUP_WIDTH), lambda b, t: (b, t, 0))
    in_specs = [uspec(part_q), uspec(part_k), gspec, gspec]
    args = [u, u, gq.reshape(1, HEAD_DIM), gk.reshape(1, HEAD_DIM)]
    if rope:
        in_specs += [tspec, tspec, tspec]
        args += list(rope_tabs)
    return pl.pallas_call(
        functools.partial(_qk_prep_kernel, rope=rope),
        grid=(B, L // tl),
        in_specs=in_specs,
        out_specs=[ospec, ospec],
        out_shape=[jax.ShapeDtypeStruct((B, L, GROUP_WIDTH), out_dtype)] * 2,
        compiler_params=_cparams("parallel", "parallel"),
        name="qk_prep_rope" if rope else "qk_prep",
    )(*args)


SWA_Q = 128
SWA_K = 2 * SWA_Q
SWA_RADIUS = 64
SWA_ROWS = 256


def _swa_kernel(q_ref, k_ref, v_ref, o_ref, ob_ref, ls_ref, *, L, T):
    t0 = pl.program_id(2) * T
    diff = (lax.broadcasted_iota(jnp.int32, (SWA_Q, SWA_K), 1)
            - lax.broadcasted_iota(jnp.int32, (SWA_Q, SWA_K), 0))

    for bi, (window, dil) in enumerate(DILATED_CONFIGS):
        assert window // (2 * dil) == SWA_RADIUS
        span = SWA_Q * dil
        for r in range(dil):
            def body(sb, carry, r=r, dil=dil, bi=bi, span=span):
                loc = sb * span
                base = jnp.clip(t0 + loc - SWA_RADIUS * dil, 0, L - SWA_K * dil)
                delta = (t0 + loc - base) // dil
                if dil == 1:
                    qs = pl.ds(pl.multiple_of(loc, SWA_Q), SWA_Q)
                    ks = pl.ds(pl.multiple_of(base, SWA_RADIUS), SWA_K)
                else:
                    qs = pl.ds(loc + r, SWA_Q, stride=dil)
                    ks = pl.ds(base + r, SWA_K, stride=dil)
                q = q_ref[0, qs, :].astype(BF16)
                k = k_ref[0, ks, :].astype(BF16)
                v = v_ref[0, ks, :].astype(BF16)
                s = _dot_nt(q, k) * ATTN_SCALE
                valid = jnp.abs(diff - delta) <= SWA_RADIUS
                s = jnp.where(valid, s, MASK_VALUE)
                m = jnp.max(s, axis=-1, keepdims=True)
                e = jnp.where(valid, jnp.exp(s - m), 0.0)
                den = jnp.sum(e, axis=-1, keepdims=True)
                ob_ref[bi, qs, :] = _dot(e.astype(BF16), v) / den
                ls_ref[bi, qs, :] = jnp.broadcast_to(m + jnp.log(den), (SWA_Q, HEAD_DIM))
                return carry
            lax.fori_loop(0, T // span, body, 0)

    def combine(c, carry):
        rows = pl.ds(pl.multiple_of(c * SWA_ROWS, SWA_ROWS), SWA_ROWS)
        l0, l1, l2 = ls_ref[0, rows, :], ls_ref[1, rows, :], ls_ref[2, rows, :]
        mx = jnp.maximum(l0, jnp.maximum(l1, l2))
        w0, w1, w2 = jnp.exp(l0 - mx), jnp.exp(l1 - mx), jnp.exp(l2 - mx)
        num = w0 * ob_ref[0, rows, :] + w1 * ob_ref[1, rows, :] + w2 * ob_ref[2, rows, :]
        o_ref[0, rows, :] = (num / (w0 + w1 + w2)).astype(o_ref.dtype)
        return carry
    lax.fori_loop(0, T // SWA_ROWS, combine, 0)


def _swa_attention(qn, kn, u, T=2048):
    B, L, _ = qn.shape
    max_dil = max(d for _, d in DILATED_CONFIGS)
    assert L % T == 0 and T % (SWA_Q * max_dil) == 0 and L >= SWA_K * max_dil
    nb = len(DILATED_CONFIGS)
    return pl.pallas_call(
        functools.partial(_swa_kernel, L=L, T=T),
        grid=(B, N_HEADS, L // T),
        in_specs=[
            pl.BlockSpec((1, T, HEAD_DIM), lambda b, h, t: (b, t, h)),
            pl.BlockSpec((1, L, HEAD_DIM), lambda b, h, t: (b, 0, h)),
            pl.BlockSpec((1, L, HEAD_DIM), lambda b, h, t: (b, 0, P_SWA_V * N_HEADS + h)),
        ],
        out_specs=pl.BlockSpec((1, T, HEAD_DIM), lambda b, h, t: (b, t, h)),
        out_shape=jax.ShapeDtypeStruct((B, L, GROUP_WIDTH), BF16),
        scratch_shapes=[pltpu.VMEM((nb, T, HEAD_DIM), F32), pltpu.VMEM((nb, T, HEAD_DIM), F32)],
        compiler_params=_cparams("parallel", "parallel", "arbitrary"),
        name="swa_attention",
    )(qn, kn, u)


NA_KEYS = NA_KH * GRID_W


def _na_bias_table(rpb):
    cq = jnp.arange(GRID_W)
    ck = jnp.arange(GRID_W)
    c0 = jnp.clip(cq - NA_KW // 2, 0, GRID_W - NA_KW)
    col_ok = (ck[None, :] >= c0[:, None]) & (ck[None, :] < c0[:, None] + NA_KW)
    col_off = jnp.clip(ck[None, :] - cq[:, None], -(NA_KW - 1), NA_KW - 1) + (NA_KW - 1)
    row_off = jnp.arange(NA_KH)[:, None] + jnp.arange(NA_KH)[None, :]
    bias = rpb.astype(F32)[:, row_off]
    bias = bias[..., col_off]
    bias = bias.transpose(0, 1, 3, 2, 4)
    bias = jnp.where(col_ok[None, None, :, None, :], bias, MASK_VALUE)
    return bias.reshape(rpb.shape[0], NA_KH, GRID_W, NA_KEYS)


def _na_kernel(q_ref, k_ref, v_ref, bias_ref, o_ref, *, rows, R):
    t = pl.program_id(2)

    def body(i, carry):
        r = t * R + i
        r0 = jnp.clip(r - NA_KH // 2, 0, rows - NA_KH)
        shift = r0 - r + (NA_KH - 1)
        qs = pl.ds(pl.multiple_of(i * GRID_W, GRID_W), GRID_W)
        ks = pl.ds(pl.multiple_of(r0 * GRID_W, GRID_W), NA_KEYS)
        s = _dot_nt(q_ref[0, qs, :], k_ref[0, ks, :]) * ATTN_SCALE + bias_ref[0, shift]
        m = jnp.max(s, axis=-1, keepdims=True)
        e = jnp.exp(s - m)
        den = jnp.sum(e, axis=-1, keepdims=True)
        o_ref[0, qs, :] = (_dot(e.astype(BF16), v_ref[0, ks, :]) / den).astype(o_ref.dtype)
        return carry
    lax.fori_loop(0, R, body, 0)


def _na_attention(qn, kn, vn, bias, R=16):
    B, L, _ = qn.shape
    rows = L // GRID_W
    assert rows >= NA_KH and rows % R == 0
    T = R * GRID_W
    return pl.pallas_call(
        functools.partial(_na_kernel, rows=rows, R=R),
        grid=(B, N_HEADS, rows // R),
        in_specs=[
            pl.BlockSpec((1, T, HEAD_DIM), lambda b, h, t: (b, t, h)),
            pl.BlockSpec((1, L, HEAD_DIM), lambda b, h, t: (b, 0, h)),
            pl.BlockSpec((1, L, HEAD_DIM), lambda b, h, t: (b, 0, h)),
            pl.BlockSpec((1, NA_KH, GRID_W, NA_KEYS), lambda b, h, t: (h, 0, 0, 0)),
        ],
        out_specs=pl.BlockSpec((1, T, HEAD_DIM), lambda b, h, t: (b, t, h)),
        out_shape=jax.ShapeDtypeStruct((B, L, GROUP_WIDTH), BF16),
        compiler_params=_cparams("parallel", "parallel", "arbitrary"),
        name="na_attention",
    )(qn, kn, vn, bias)


def _cast_kernel(x_ref, o_ref):
    o_ref[...] = x_ref[...].astype(o_ref.dtype)


def _cast_part(u, part, dtype, tl=1024):
    B, L, _ = u.shape
    return pl.pallas_call(
        _cast_kernel,
        grid=(B, L // tl),
        in_specs=[pl.BlockSpec((1, tl, GROUP_WIDTH), lambda b, t: (b, t, part))],
        out_specs=pl.BlockSpec((1, tl, GROUP_WIDTH), lambda b, t: (b, t, 0)),
        out_shape=jax.ShapeDtypeStruct((B, L, GROUP_WIDTH), dtype),
        compiler_params=_cparams("parallel", "parallel"),
        name="cast_part",
    )(u)


def _scan_masks(fwd):
    ri = lax.broadcasted_iota(jnp.int32, (CHUNK, CHUNK), 0)
    ci = lax.broadcasted_iota(jnp.int32, (CHUNK, CHUNK), 1)
    dist = (ci - ri) * jnp.where(fwd, 1, -1)
    return dist <= 0, dist < 0, ci


def _dir_tile(t, d, n):
    return t + d * (n - 1 - 2 * t)


def _hgrn_kernel(q_ref, f_ref, v_ref, lg_ref, o_ref, s_ref, kk_ref, b_ref, *, layer, T):
    d = pl.program_id(2)
    fwd = d == 0

    @pl.when(pl.program_id(3) == 0)
    def _():
        s_ref[...] = jnp.zeros_like(s_ref)

    lg = lg_ref[:, 0, 0, :]
    ex = jnp.exp(lg - jnp.max(lg, axis=0, keepdims=True))
    p = ex / jnp.sum(ex, axis=0, keepdims=True)
    cs = p[0:1]
    for i in range(1, layer + 1):
        cs = cs + p[i:i + 1]
    lb = cs - p[0:1]

    incl, _, ci = _scan_masks(fwd)
    tri = jnp.where(incl, 1.0, 0.0).astype(BF16)
    n_chunks = T // CHUNK
    n_sub = CHUNK // SUB

    def chunk(step, carry):
        c = jnp.where(fwd, step, n_chunks - 1 - step)
        rows = pl.ds(pl.multiple_of(c * CHUNK, CHUNK), CHUNK)
        xf = f_ref[0, rows, :]
        v = v_ref[0, rows, :].astype(BF16)
        qq = _silu(q_ref[0, rows, :])
        kk = (1.0 - lb) * _sigmoid(-xf)
        g = jnp.log(jnp.maximum(lb + (1.0 - lb) * _sigmoid(xf), LOG_FLOOR))
        b = _cumsum_rows(tri, g)
        tot = jnp.where(fwd, b[CHUNK - 1:CHUNK], b[0:1])
        bx = b - g

        refs, worst = [], None
        for i in range(n_sub):
            lo, hi = i * SUB, (i + 1) * SUB
            r_i = jnp.where(fwd, bx[lo:lo + 1], bx[hi - 1:hi])
            end = jnp.where(fwd, b[hi - 1:hi], b[lo:lo + 1])
            refs.append(r_i)
            worst = (r_i - end) if worst is None else jnp.maximum(worst, r_i - end)
        factored_ok = jnp.max(worst) <= EXP_CLAMP

        def factored():
            blocks = []
            for i in range(n_sub):
                lo, hi = i * SUB, (i + 1) * SUB
                qi = qq[lo:hi] * jnp.exp(b[lo:hi] - refs[i])
                kt = kk * jnp.exp(jnp.minimum(refs[i] - b, EXP_CLAMP))
                blocks.append(_dot_nt(qi.astype(BF16), kt.astype(BF16)))
            return jnp.concatenate(blocks, axis=0)

        kk_ref[...] = kk
        b_ref[...] = b

        def exact():
            def col(j, a):
                kj = kk_ref[pl.ds(j, 1), :]
                bj = b_ref[pl.ds(j, 1), :]
                tt = qq * kj * jnp.exp(jnp.minimum(b - bj, 0.0))
                return jnp.where(ci == j, jnp.sum(tt, axis=-1, keepdims=True), a)
            return lax.fori_loop(0, CHUNK, col, jnp.zeros((CHUNK, CHUNK), F32))

        a = jnp.where(incl, lax.cond(factored_ok, factored, exact), 0.0)

        st = s_ref[...]
        o = _dot_nt((qq * jnp.exp(b)).astype(BF16), st.astype(BF16)) + _dot(a.astype(BF16), v)
        kdec = (kk * jnp.exp(tot - b)).astype(BF16)
        s_ref[...] = jnp.exp(tot) * st + _dot_tn(v, kdec)
        o_ref[0, 0, rows, :] = o
        return carry
    lax.fori_loop(0, n_chunks, chunk, 0)


def _hgrn_scan(u, lb_logits, layer, T=1024):
    B, L, _ = u.shape
    n = L // T

    def uspec(part_of_dir):
        return pl.BlockSpec(
            (1, T, HEAD_DIM),
            lambda b, h, d, t: (b, _dir_tile(t, d, n), part_of_dir(d) * N_HEADS + h))

    return pl.pallas_call(
        functools.partial(_hgrn_kernel, layer=layer, T=T),
        grid=(B, N_HEADS, 2, n),
        in_specs=[
            uspec(lambda d: P_HG_Q),
            uspec(lambda d: P_HG_FF + d),
            uspec(lambda d: P_HG_I),
            pl.BlockSpec((DEPTH, 1, 1, HEAD_DIM), lambda b, h, d, t: (0, d, 0, h)),
        ],
        out_specs=pl.BlockSpec((1, 1, T, HEAD_DIM),
                               lambda b, h, d, t: (d, b, _dir_tile(t, d, n), h)),
        out_shape=jax.ShapeDtypeStruct((2, B, L, GROUP_WIDTH), F32),
        scratch_shapes=[pltpu.VMEM((HEAD_DIM, HEAD_DIM), F32),
                        pltpu.VMEM((CHUNK, HEAD_DIM), F32),
                        pltpu.VMEM((CHUNK, HEAD_DIM), F32)],
        compiler_params=_cparams("parallel", "parallel", "parallel", "arbitrary"),
        name="hgrn_scan",
    )(u, u, u, lb_logits.reshape(DEPTH, 2, 1, GROUP_WIDTH))


def _gated_norm_kernel(of_ref, ob_ref, z_ref, g_ref, y_ref):
    for h in range(N_HEADS):
        cols = slice(h * HEAD_DIM, (h + 1) * HEAD_DIM)
        o = of_ref[0, 0, :, cols] + ob_ref[0, 0, :, cols]
        y_ref[0, :, cols] = (_rms(o) * g_ref[...] * _silu(z_ref[0, :, cols])).astype(y_ref.dtype)


def _gated_norm(o2, u, part_z, g, tl=512):
    _, B, L, _ = o2.shape
    return pl.pallas_call(
        _gated_norm_kernel,
        grid=(B, L // tl),
        in_specs=[
            pl.BlockSpec((1, 1, tl, GROUP_WIDTH), lambda b, t: (0, b, t, 0)),
            pl.BlockSpec((1, 1, tl, GROUP_WIDTH), lambda b, t: (1, b, t, 0)),
            pl.BlockSpec((1, tl, GROUP_WIDTH), lambda b, t: (b, t, part_z)),
            pl.BlockSpec((1, HEAD_DIM), lambda b, t: (0, 0)),
        ],
        out_specs=pl.BlockSpec((1, tl, GROUP_WIDTH), lambda b, t: (b, t, 0)),
        out_shape=jax.ShapeDtypeStruct((B, L, GROUP_WIDTH), BF16),
        compiler_params=_cparams("parallel", "parallel"),
        name="gated_norm",
    )(o2, o2, u, g.reshape(1, HEAD_DIM))


DN_HALO = 8


def _dn_prep_kernel(*refs, tl):
    w_ref = refs[9]
    outs = refs[10:13]
    t = pl.program_id(1)
    has_prev = t > 0
    has_next = t < pl.num_programs(1) - 1
    ext_rows = tl + 2 * DN_HALO
    for kind in range(3):
        prev_ref, cur_ref, next_ref = refs[3 * kind:3 * kind + 3]
        for h in range(N_HEADS):
            cols = slice(h * HEAD_DIM, (h + 1) * HEAD_DIM)
            wcols = slice(kind * GROUP_WIDTH + h * HEAD_DIM, kind * GROUP_WIDTH + (h + 1) * HEAD_DIM)
            ext = jnp.concatenate([
                jnp.where(has_prev, prev_ref[0, :, cols], 0.0),
                cur_ref[0, :, cols],
                jnp.where(has_next, next_ref[0, :, cols], 0.0)], axis=0)
            acc = None
            for j in range(DN_CONV):
                tap = pltpu.roll(ext, (DN_CONV // 2 - j) % ext_rows, 0)[DN_HALO:DN_HALO + tl]
                term = tap * w_ref[j:j + 1, wcols]
                acc = term if acc is None else acc + term
            y = _silu(acc)
            if kind < 2:
                y = y * lax.rsqrt(jnp.sum(y * y, axis=-1, keepdims=True) + NORM_EPS)
            if kind == 0:
                y = y * ATTN_SCALE
            outs[kind][0, :, cols] = y


def _dn_prep(u, conv_w, tl=512):
    B, L, _ = u.shape
    nh = tl // DN_HALO
    last = L // DN_HALO - 1
    in_specs = []
    for kind in range(3):
        part = P_DN_QKV + kind
        in_specs += [
            pl.BlockSpec((1, DN_HALO, GROUP_WIDTH),
                         lambda b, t, part=part: (b, jnp.maximum(t * nh - 1, 0), part)),
            pl.BlockSpec((1, tl, GROUP_WIDTH), lambda b, t, part=part: (b, t, part)),
            pl.BlockSpec((1, DN_HALO, GROUP_WIDTH),
                         lambda b, t, part=part: (b, jnp.minimum((t + 1) * nh, last), part)),
        ]
    in_specs.append(pl.BlockSpec((DN_CONV, 3 * GROUP_WIDTH), lambda b, t: (0, 0)))
    ospec = pl.BlockSpec((1, tl, GROUP_WIDTH), lambda b, t: (b, t, 0))
    return pl.pallas_call(
        functools.partial(_dn_prep_kernel, tl=tl),
        grid=(B, L // tl),
        in_specs=in_specs,
        out_specs=[ospec, ospec, ospec],
        out_shape=[jax.ShapeDtypeStruct((B, L, GROUP_WIDTH), F32)] * 3,
        compiler_params=_cparams("parallel", "parallel"),
        name="dn_prep",
    )(*([u] * 9), conv_w)


def _softplus(x):
    return jnp.maximum(x, 0.0) + jnp.log1p(jnp.exp(-jnp.abs(x)))


def _dn_kernel(q_ref, k_ref, v_ref, ab_ref, alog_ref, dtb_ref, o_ref, s_ref, *, T):
    h = pl.program_id(1)
    d = pl.program_id(2)
    fwd = d == 0

    @pl.when(pl.program_id(3) == 0)
    def _():
        s_ref[...] = jnp.zeros_like(s_ref)

    incl, strict, _ = _scan_masks(fwd)
    tri = jnp.where(incl, 1.0, 0.0).astype(BF16)
    eye = jnp.where(lax.broadcasted_iota(jnp.int32, (CHUNK, CHUNK), 0)
                    == lax.broadcasted_iota(jnp.int32, (CHUNK, CHUNK), 1), 1.0, 0.0)
    ones = jnp.ones((CHUNK, CHUNK), BF16)
    lane = lax.broadcasted_iota(jnp.int32, (1, HEAD_DIM), 1)
    col_a = d * N_HEADS + h
    col_b = 2 * N_HEADS + col_a
    neg_a = -jnp.exp(alog_ref[...])
    n_chunks = T // CHUNK

    def chunk(step, carry):
        c = jnp.where(fwd, step, n_chunks - 1 - step)
        rows = pl.ds(pl.multiple_of(c * CHUNK, CHUNK), CHUNK)
        q, k, v = q_ref[0, rows, :], k_ref[0, rows, :], v_ref[0, rows, :]
        ab = ab_ref[0, rows, :]
        g_all = neg_a * _softplus(ab + dtb_ref[...])
        g_col = jnp.sum(jnp.where(lane == col_a, g_all, 0.0), axis=-1, keepdims=True)
        beta = jnp.sum(jnp.where(lane == col_b, _sigmoid(ab), 0.0), axis=-1, keepdims=True)
        G = _cumsum_rows(tri, jnp.broadcast_to(g_col, (CHUNK, HEAD_DIM)))
        g_rows = _cumsum_rows(ones, jnp.where(eye > 0.0, G[:, 0:CHUNK], 0.0))
        gamma = jnp.where(incl, jnp.exp(jnp.minimum(G[:, 0:CHUNK] - g_rows, 0.0)), 0.0)
        kb = k * beta
        kbf = k.astype(BF16)
        n_mat = jnp.where(strict, _dot_nt(kb.astype(BF16), kbf) * gamma, 0.0)

        x = eye - n_mat
        nk = _dot_hi(n_mat, n_mat)
        order = 2
        while True:
            x = x + _dot_hi(x, nk)
            order *= 2
            if order >= CHUNK:
                break
            nk = _dot_hi(nk, nk)

        eg = jnp.exp(G)
        w = _dot_hi(x, kb * eg)
        uu = _dot_hi(x, v * beta)
        a_qk = _dot_nt(q.astype(BF16), kbf) * gamma
        g_last = jnp.where(fwd, G[CHUNK - 1:CHUNK], G[0:1])
        k_dec = (k * jnp.exp(g_last - G)).astype(BF16)

        s = s_ref[...]
        sb = s.astype(BF16)
        v_new = uu - _dot(w.astype(BF16), sb)
        v_new_b = v_new.astype(BF16)
        o_ref[0, 0, rows, :] = _dot((q * eg).astype(BF16), sb) + _dot(a_qk.astype(BF16), v_new_b)
        s_ref[...] = jnp.exp(g_last[:, 0:1]) * s + _dot_tn(k_dec, v_new_b)
        return carry
    lax.fori_loop(0, n_chunks, chunk, 0)


def _dn_scan(qh, kh, vh, uab, a_log, dt_bias, T=1024):
    B, L, _ = qh.shape
    n = L // T
    pad = HEAD_DIM - 2 * N_HEADS
    alog = jnp.pad(a_log.astype(F32).reshape(1, 2 * N_HEADS), ((0, 0), (0, pad)))
    dtb = jnp.pad(dt_bias.astype(F32).reshape(1, 2 * N_HEADS), ((0, 0), (0, pad)))
    hspec = pl.BlockSpec((1, T, HEAD_DIM), lambda b, h, d, t: (b, _dir_tile(t, d, n), h))
    pspec = pl.BlockSpec((1, HEAD_DIM), lambda b, h, d, t: (0, 0))
    return pl.pallas_call(
        functools.partial(_dn_kernel, T=T),
        grid=(B, N_HEADS, 2, n),
        in_specs=[hspec, hspec, hspec,
                  pl.BlockSpec((1, T, HEAD_DIM), lambda b, h, d, t: (b, _dir_tile(t, d, n), 0)),
                  pspec, pspec],
        out_specs=pl.BlockSpec((1, 1, T, HEAD_DIM),
                               lambda b, h, d, t: (d, b, _dir_tile(t, d, n), h)),
        out_shape=jax.ShapeDtypeStruct((2, B, L, GROUP_WIDTH), F32),
        scratch_shapes=[pltpu.VMEM((HEAD_DIM, HEAD_DIM), F32)],
        compiler_params=_cparams("parallel", "parallel", "parallel", "arbitrary"),
        name="dn_scan",
    )(qh, kh, vh, uab, alog, dtb)


def _encoder_layer(x, mod, layer, p, rope_tabs):
    u, uab = _in_proj(x, p["norm1_g"][layer], mod, p["w_in_main"][layer], p["w_in_ab"][layer])

    swa_q, swa_k = _qk_prep(u, P_SWA_Q, P_SWA_K, p["swa_q_norm"][layer], p["swa_k_norm"][layer],
                            rope_tabs, F32)
    y_swa = _swa_attention(swa_q, swa_k, u)

    o_hg = _hgrn_scan(u, p["hgrn_lb_logits"], layer)
    y_hg = _gated_norm(o_hg, u, P_HG_G, p["hgrn_norm_g"][layer])

    na_q, na_k = _qk_prep(u, P_NA_Q, P_NA_K, p["na_q_norm"][layer], p["na_k_norm"][layer], None, BF16)
    na_v = _cast_part(u, P_NA_V, BF16)
    y_na = _na_attention(na_q, na_k, na_v, p["na_bias"][layer])

    dq, dk, dv = _dn_prep(u, p["dn_conv_w"][layer])
    o_dn = _dn_scan(dq, dk, dv, uab, p["dn_a_log"][layer], p["dn_dt_bias"][layer])
    y_dn = _gated_norm(o_dn, u, P_DN_Z, p["dn_norm_g"][layer])

    x = _out_proj(x, mod, (y_swa, y_hg, y_na, y_dn), p["w_out"][layer])
    return _mlp(x, p["norm2_g"][layer], mod, p["w_mlp_in"][layer], p["w_mlp_out"][layer])


def _run_trunk(x, mod_all, p):
    B, L, _ = x.shape
    rope_tabs = _rope_tables(L)
    for layer in range(DEPTH):
        x = _encoder_layer(x, mod_all[layer].reshape(B, 6, D_MODEL), layer, p, rope_tabs)
    return x


def kernel(x_prompt, x_sample, c_prompt, c_sample, norm1_g, norm2_g, ada_w, ada_b, w_in, w_out,
           swa_q_norm, swa_k_norm, hgrn_lb_logits, hgrn_norm_g, na_q_norm, na_k_norm, na_rpb,
           dn_conv_w, dn_a_log, dn_dt_bias, dn_norm_g, w_mlp_in, w_mlp_out):
    bp, bs = c_prompt.shape[0], c_sample.shape[0]
    pad_rows = -(bp + bs) % 8
    c_all = jnp.concatenate([c_prompt, c_sample, jnp.zeros((pad_rows, D_MODEL), F32)], axis=0)
    mod_all = _modulation(c_all, ada_w, ada_b)

    p = dict(
        norm1_g=norm1_g, norm2_g=norm2_g,
        w_in_main=w_in[:, :, :IN_MAIN].astype(BF16),
        w_in_ab=jnp.pad(w_in[:, :, IN_MAIN:], ((0, 0), (0, 0), (0, HEAD_DIM - 4 * N_HEADS))).astype(BF16),
        w_out=w_out.astype(BF16),
        swa_q_norm=swa_q_norm, swa_k_norm=swa_k_norm,
        hgrn_lb_logits=hgrn_lb_logits.astype(F32), hgrn_norm_g=hgrn_norm_g,
        na_q_norm=na_q_norm, na_k_norm=na_k_norm,
        na_bias=jnp.stack([_na_bias_table(na_rpb[l]) for l in range(DEPTH)]),
        dn_conv_w=dn_conv_w, dn_a_log=dn_a_log, dn_dt_bias=dn_dt_bias, dn_norm_g=dn_norm_g,
        w_mlp_in=w_mlp_in.astype(BF16), w_mlp_out=w_mlp_out.astype(BF16),
    )
    y_prompt = _run_trunk(x_prompt, mod_all[:, :bp], p)
    y_sample = _run_trunk(x_sample, mod_all[:, bp:bp + bs], p)
    return (y_prompt, y_sample)
```

```python
import functools
import math

import jax
import jax.numpy as jnp
from jax import lax
from jax.experimental import pallas as pl
from jax.experimental.pallas import tpu as pltpu

F32 = jnp.float32
BF16 = jnp.bfloat16

D_MODEL = 2048
DEPTH = 4
HEAD_DIM = 128
N_HEADS = 4
GROUP_WIDTH = N_HEADS * HEAD_DIM
IN_MAIN = 15 * GROUP_WIDTH
D_FF = 4 * D_MODEL
NORM_EPS = 1e-6
ROPE_THETA = 500000.0
ROPE_DIM = HEAD_DIM // 4
DILATED_CONFIGS = ((128, 1), (512, 4), (2048, 16))
GRID_W = 64
NA_KH = 8
NA_KW = 16
CHUNK = 64
SUB = 16
DN_CONV = 5
MASK_VALUE = -1e30
LOG_FLOOR = 1e-30
ATTN_SCALE = HEAD_DIM ** -0.5
EXP_CLAMP = 60.0
HIGHEST = lax.Precision.HIGHEST

P_SWA_Q, P_SWA_K, P_SWA_V = 0, 1, 2
P_HG_Q, P_HG_FF, P_HG_FB, P_HG_I, P_HG_G = 3, 4, 5, 6, 7
P_NA_Q, P_NA_K, P_NA_V = 8, 9, 10
P_DN_QKV, P_DN_Z = 11, 14

VMEM_LIMIT = 56 * 1024 * 1024


def _cparams(*sem):
    return pltpu.CompilerParams(dimension_semantics=sem, vmem_limit_bytes=VMEM_LIMIT)


def _dot(a, b):
    return jnp.dot(a, b, preferred_element_type=F32)


def _dot_nt(a, b):
    return lax.dot_general(a, b, (((1,), (1,)), ((), ())), preferred_element_type=F32)


def _dot_tn(a, b):
    return lax.dot_general(a, b, (((0,), (0,)), ((), ())), preferred_element_type=F32)


def _sigmoid(x):
    return jax.nn.sigmoid(x)


def _silu(x):
    return x * _sigmoid(x)


def _rms(x):
    return x * lax.rsqrt(jnp.mean(x * x, axis=-1, keepdims=True) + NORM_EPS)


def _split2(x):
    hi = x.astype(BF16)
    return hi, (x - hi.astype(F32)).astype(BF16)


def _split3(x):
    hi = x.astype(BF16)
    r1 = x - hi.astype(F32)
    mid = r1.astype(BF16)
    return hi, mid, (r1 - mid.astype(F32)).astype(BF16)


def _stack3(x):
    hi, mid, lo = _split3(x)
    return jnp.concatenate([hi, mid, lo, jnp.zeros_like(hi)], axis=0)


def _iota2(shape, axis):
    return lax.broadcasted_iota(jnp.int32, shape, axis)


def _tri4(fwd):
    ri, ci = _iota2((CHUNK, 4 * CHUNK), 0), _iota2((CHUNK, 4 * CHUNK), 1)
    cm = ci & (CHUNK - 1)
    order = (cm <= ri) if fwd else (cm >= ri)
    return jnp.where(order & (ci < 3 * CHUNK), 1.0, 0.0).astype(BF16)


def _eye4():
    ri, ci = _iota2((4 * CHUNK, 2 * CHUNK), 0), _iota2((4 * CHUNK, 2 * CHUNK), 1)
    same = (ri & (CHUNK - 1)) == (ci & (CHUNK - 1))
    return jnp.where(same & (ri < 3 * CHUNK), 1.0, 0.0).astype(BF16)


def _order_masks(fwd, width):
    ri, ci = _iota2((CHUNK, width), 0), _iota2((CHUNK, width), 1) & (CHUNK - 1)
    if fwd:
        return ci <= ri, ci < ri
    return ci >= ri, ci > ri


def _mod_kernel(c_ref, w_ref, b_ref, o_ref):
    a = _silu(c_ref[...]).astype(BF16)
    o_ref[0] = _dot(a, w_ref[0].astype(BF16)) + b_ref[0]


def _modulation(c_all, ada_w, ada_b):
    rows, d = c_all.shape
    n = ada_w.shape[-1]
    tn = 1024
    return pl.pallas_call(
        _mod_kernel,
        grid=(DEPTH, n // tn),
        in_specs=[
            pl.BlockSpec((rows, d), lambda l, j: (0, 0)),
            pl.BlockSpec((1, d, tn), lambda l, j: (l, 0, j)),
            pl.BlockSpec((1, 1, tn), lambda l, j: (l, 0, j)),
        ],
        out_specs=pl.BlockSpec((1, rows, tn), lambda l, j: (l, 0, j)),
        out_shape=jax.ShapeDtypeStruct((DEPTH, rows, n), F32),
        compiler_params=_cparams("parallel", "parallel"),
        name="adaln_mod",
    )(c_all, ada_w, ada_b.reshape(DEPTH, 1, n))


NORM_ROWS = 256


def _norm_mod_store(h_ref, x_ref, g, scale, shift, tm):
    def body(c, carry):
        rows = pl.ds(pl.multiple_of(c * NORM_ROWS, NORM_ROWS), NORM_ROWS)
        y = _rms(x_ref[0, rows, :]) * g
        h_ref[rows, :] = (y * (1.0 + scale) + shift).astype(BF16)
        return carry
    lax.fori_loop(0, tm // NORM_ROWS, body, 0)


def _in_proj_kernel(x_ref, g_ref, mod_ref, w_ref, wab_ref, u_ref, uab_ref, h_ref, *, tm):
    @pl.when(pl.program_id(2) == 0)
    def _():
        _norm_mod_store(h_ref, x_ref, g_ref[...], mod_ref[0, 1:2, :], mod_ref[0, 0:1, :], tm)
        uab_ref[0] = _dot(h_ref[...], wab_ref[...])
    u_ref[0] = _dot(h_ref[...], w_ref[...])


def _in_proj(x, g, mod, w_main, w_ab, tm=512, tn=1536):
    B, L, D = x.shape
    N = w_main.shape[1]
    return pl.pallas_call(
        functools.partial(_in_proj_kernel, tm=tm),
        grid=(B, L // tm, N // tn),
        in_specs=[
            pl.BlockSpec((1, tm, D), lambda b, i, j: (b, i, 0)),
            pl.BlockSpec((1, D), lambda b, i, j: (0, 0)),
            pl.BlockSpec((1, 6, D), lambda b, i, j: (b, 0, 0)),
            pl.BlockSpec((D, tn), lambda b, i, j: (0, j)),
            pl.BlockSpec((D, HEAD_DIM), lambda b, i, j: (0, 0)),
        ],
        out_specs=[
            pl.BlockSpec((1, tm, tn), lambda b, i, j: (b, i, j)),
            pl.BlockSpec((1, tm, HEAD_DIM), lambda b, i, j: (b, i, 0)),
        ],
        out_shape=[jax.ShapeDtypeStruct((B, L, N), F32),
                   jax.ShapeDtypeStruct((B, L, HEAD_DIM), F32)],
        scratch_shapes=[pltpu.VMEM((tm, D), BF16)],
        compiler_params=_cparams("parallel", "parallel", "arbitrary"),
        name="in_proj",
    )(x, g.reshape(1, D), mod, w_main, w_ab)


def _out_proj_kernel(x_ref, mod_ref, y0_ref, y1_ref, y2_ref, y3_ref, w_ref, o_ref):
    acc = None
    for m, y_ref in enumerate((y0_ref, y1_ref, y2_ref, y3_ref)):
        part = _dot(y_ref[0], w_ref[m * GROUP_WIDTH:(m + 1) * GROUP_WIDTH, :])
        acc = part if acc is None else acc + part
    o_ref[0] = x_ref[0] + mod_ref[0, 2:3, :] * acc


def _out_proj(x, mod, ys, w_out, tm=512):
    B, L, D = x.shape
    yspec = pl.BlockSpec((1, tm, GROUP_WIDTH), lambda b, i: (b, i, 0))
    return pl.pallas_call(
        _out_proj_kernel,
        grid=(B, L // tm),
        in_specs=[
            pl.BlockSpec((1, tm, D), lambda b, i: (b, i, 0)),
            pl.BlockSpec((1, 6, D), lambda b, i: (b, 0, 0)),
            yspec, yspec, yspec, yspec,
            pl.BlockSpec((D, D), lambda b, i: (0, 0)),
        ],
        out_specs=pl.BlockSpec((1, tm, D), lambda b, i: (b, i, 0)),
        out_shape=jax.ShapeDtypeStruct((B, L, D), F32),
        compiler_params=_cparams("parallel", "parallel"),
        name="out_proj",
    )(x, mod, *ys, w_out)


def _mlp_kernel(x_ref, g_ref, mod_ref, w1_ref, w2_ref, o_ref, h_ref, *, tm):
    j = pl.program_id(2)

    @pl.when(j == 0)
    def _():
        _norm_mod_store(h_ref, x_ref, g_ref[...], mod_ref[0, 4:5, :], mod_ref[0, 3:4, :], tm)

    hid = jnp.square(jnp.maximum(_dot(h_ref[...], w1_ref[...]), 0.0)).astype(BF16)
    part = _dot(hid, w2_ref[...])

    @pl.when(j == 0)
    def _():
        o_ref[0] = part

    @pl.when(j > 0)
    def _():
        o_ref[0] += part

    @pl.when(j == pl.num_programs(2) - 1)
    def _():
        o_ref[0] = x_ref[0] + mod_ref[0, 5:6, :] * o_ref[0]


def _mlp(x, g, mod, w1, w2, tm=512, tf=1024):
    B, L, D = x.shape
    F = w1.shape[1]
    return pl.pallas_call(
        functools.partial(_mlp_kernel, tm=tm),
        grid=(B, L // tm, F // tf),
        in_specs=[
            pl.BlockSpec((1, tm, D), lambda b, i, j: (b, i, 0)),
            pl.BlockSpec((1, D), lambda b, i, j: (0, 0)),
            pl.BlockSpec((1, 6, D), lambda b, i, j: (b, 0, 0)),
            pl.BlockSpec((D, tf), lambda b, i, j: (0, j)),
            pl.BlockSpec((tf, D), lambda b, i, j: (j, 0)),
        ],
        out_specs=pl.BlockSpec((1, tm, D), lambda b, i, j: (b, i, 0)),
        out_shape=jax.ShapeDtypeStruct((B, L, D), F32),
        scratch_shapes=[pltpu.VMEM((tm, D), BF16)],
        compiler_params=_cparams("parallel", "parallel", "arbitrary"),
        name="mlp",
    )(x, g.reshape(1, D), mod, w1, w2)


def _qk_prep_kernel(*refs, rope):
    if rope:
        q_ref, k_ref, gq_ref, gk_ref, c_ref, s1_ref, s2_ref, qo_ref, ko_ref = refs
    else:
        q_ref, k_ref, gq_ref, gk_ref, qo_ref, ko_ref = refs
    for src, g_ref, dst in ((q_ref, gq_ref, qo_ref), (k_ref, gk_ref, ko_ref)):
        for h in range(N_HEADS):
            cols = slice(h * HEAD_DIM, (h + 1) * HEAD_DIM)
            y = _rms(src[0, :, cols]) * g_ref[...]
            if rope:
                y = (y * c_ref[...] + pltpu.roll(y, ROPE_DIM // 2, 1) * s1_ref[...]
                     + pltpu.roll(y, HEAD_DIM - ROPE_DIM // 2, 1) * s2_ref[...])
            dst[0, :, cols] = y.astype(dst.dtype)


def _rope_tables(L):
    half = ROPE_DIM // 2
    inv_freq = jnp.power(ROPE_THETA, -jnp.arange(half, dtype=F32) / half)
    ang = jnp.arange(L).astype(F32)[:, None] * inv_freq[None, :]
    cos, sin = jnp.cos(ang), jnp.sin(ang)
    zeros = jnp.zeros((L, HEAD_DIM - ROPE_DIM), F32)
    zh = jnp.zeros((L, half), F32)
    c = jnp.concatenate([cos, cos, jnp.ones((L, HEAD_DIM - ROPE_DIM), F32)], axis=1)
    s1 = jnp.concatenate([zh, sin, zeros], axis=1)
    s2 = jnp.concatenate([-sin, zh, zeros], axis=1)
    return c, s1, s2


def _qk_prep(u, part_q, part_k, gq, gk, rope_tabs, out_dtype, tl=512):
    B, L, _ = u.shape
    rope = rope_tabs is not None
    uspec = lambda part: pl.BlockSpec((1, tl, GROUP_WIDTH), lambda b, t, part=part: (b, t, part))
    gspec = pl.BlockSpec((1, HEAD_DIM), lambda b, t: (0, 0))
    tspec = pl.BlockSpec((tl, HEAD_DIM), lambda b, t: (t, 0))
    ospec = pl.BlockSpec((1, tl, GROUP_WIDTH), lambda b, t: (b, t, 0))
    in_specs = [uspec(part_q), uspec(part_k), gspec, gspec]
    args = [u, u, gq.reshape(1, HEAD_DIM), gk.reshape(1, HEAD_DIM)]
    if rope:
        in_specs += [tspec, tspec, tspec]
        args += list(rope_tabs)
    return pl.pallas_call(
        functools.partial(_qk_prep_kernel, rope=rope),
        grid=(B, L // tl),
        in_specs=in_specs,
        out_specs=[ospec, ospec],
        out_shape=[jax.ShapeDtypeStruct((B, L, GROUP_WIDTH), out_dtype)] * 2,
        compiler_params=_cparams("parallel", "parallel"),
        name="qk_prep_rope" if rope else "qk_prep",
    )(*args)


SWA_Q = 128
SWA_K = 2 * SWA_Q
SWA_RADIUS = 64
SWA_ROWS = 256
SWA_GROUP = 4


def _swa_kernel(q_ref, k_ref, v_ref, o_ref, ob_ref, ls_ref, mask_ref, *, L, T):
    t0 = pl.program_id(2) * T
    diff = _iota2((SWA_Q, SWA_K), 1) - _iota2((SWA_Q, SWA_K), 0)
    for i in range(3):
        mask_ref[i] = jnp.where(jnp.abs(diff - i * SWA_RADIUS) <= SWA_RADIUS, 0.0, MASK_VALUE)

    for bi, (window, dil) in enumerate(DILATED_CONFIGS):
        assert window // (2 * dil) == SWA_RADIUS
        span = SWA_Q * dil
        n_sub = T // span

        def body(it, carry, dil=dil, bi=bi, span=span, n_sub=n_sub):
            st = []
            for j in range(SWA_GROUP):
                idx = it * SWA_GROUP + j
                r, loc = idx // n_sub, (idx % n_sub) * span
                base = jnp.clip(t0 + loc - SWA_RADIUS * dil, 0, L - SWA_K * dil)
                if dil == 1:
                    qs = pl.ds(pl.multiple_of(loc, SWA_Q), SWA_Q)
                    ks = pl.ds(pl.multiple_of(base, SWA_RADIUS), SWA_K)
                else:
                    qs = pl.ds(loc + r, SWA_Q, stride=dil)
                    ks = pl.ds(base + r, SWA_K, stride=dil)
                st.append(dict(qs=qs, ks=ks, mask=(t0 + loc - base) // (dil * SWA_RADIUS)))
            for s in st:
                sc = _dot_nt(q_ref[0, s["qs"], :].astype(BF16), k_ref[0, s["ks"], :].astype(BF16))
                s["s"] = sc * ATTN_SCALE + mask_ref[s["mask"]]
            for s in st:
                s["m"] = jnp.max(s["s"], axis=-1, keepdims=True)
                s["e"] = jnp.exp(s["s"] - s["m"])
                s["den"] = jnp.sum(s["e"], axis=-1, keepdims=True)
            for s in st:
                pv = _dot(s["e"].astype(BF16), v_ref[0, s["ks"], :].astype(BF16))
                ob_ref[bi, s["qs"], :] = pv / s["den"]
                ls_ref[bi, s["qs"], :] = jnp.broadcast_to(s["m"] + jnp.log(s["den"]), (SWA_Q, HEAD_DIM))
            return carry
        lax.fori_loop(0, T // SWA_Q // SWA_GROUP, body, 0)

    def combine(c, carry):
        rows = pl.ds(pl.multiple_of(c * SWA_ROWS, SWA_ROWS), SWA_ROWS)
        l0, l1, l2 = ls_ref[0, rows, :], ls_ref[1, rows, :], ls_ref[2, rows, :]
        mx = jnp.maximum(l0, jnp.maximum(l1, l2))
        w0, w1, w2 = jnp.exp(l0 - mx), jnp.exp(l1 - mx), jnp.exp(l2 - mx)
        num = w0 * ob_ref[0, rows, :] + w1 * ob_ref[1, rows, :] + w2 * ob_ref[2, rows, :]
        o_ref[0, rows, :] = (num / (w0 + w1 + w2)).astype(o_ref.dtype)
        return carry
    lax.fori_loop(0, T // SWA_ROWS, combine, 0)


def _swa_attention(qn, kn, u, T=2048):
    B, L, _ = qn.shape
    max_dil = max(d for _, d in DILATED_CONFIGS)
    assert L % T == 0 and T % (SWA_Q * max_dil) == 0 and L >= SWA_K * max_dil
    nb = len(DILATED_CONFIGS)
    return pl.pallas_call(
        functools.partial(_swa_kernel, L=L, T=T),
        grid=(B, N_HEADS, L // T),
        in_specs=[
            pl.BlockSpec((1, T, HEAD_DIM), lambda b, h, t: (b, t, h)),
            pl.BlockSpec((1, L, HEAD_DIM), lambda b, h, t: (b, 0, h)),
            pl.BlockSpec((1, L, HEAD_DIM), lambda b, h, t: (b, 0, P_SWA_V * N_HEADS + h)),
        ],
        out_specs=pl.BlockSpec((1, T, HEAD_DIM), lambda b, h, t: (b, t, h)),
        out_shape=jax.ShapeDtypeStruct((B, L, GROUP_WIDTH), BF16),
        scratch_shapes=[pltpu.VMEM((nb, T, HEAD_DIM), F32), pltpu.VMEM((nb, T, HEAD_DIM), F32),
                        pltpu.VMEM((3, SWA_Q, SWA_K), F32)],
        compiler_params=_cparams("parallel", "parallel", "arbitrary"),
        name="swa_attention",
    )(qn, kn, u)


NA_KEYS = NA_KH * GRID_W
NA_GROUP = 8


def _na_bias_table(rpb):
    cq = jnp.arange(GRID_W)
    ck = jnp.arange(GRID_W)
    c0 = jnp.clip(cq - NA_KW // 2, 0, GRID_W - NA_KW)
    col_ok = (ck[None, :] >= c0[:, None]) & (ck[None, :] < c0[:, None] + NA_KW)
    col_off = jnp.clip(ck[None, :] - cq[:, None], -(NA_KW - 1), NA_KW - 1) + (NA_KW - 1)
    row_off = jnp.arange(NA_KH)[:, None] + jnp.arange(NA_KH)[None, :]
    bias = rpb.astype(F32)[:, row_off]
    bias = bias[..., col_off]
    bias = bias.transpose(0, 1, 3, 2, 4)
    bias = jnp.where(col_ok[None, None, :, None, :], bias, MASK_VALUE)
    return bias.reshape(rpb.shape[0], NA_KH, GRID_W, NA_KEYS)


def _na_kernel(q_ref, k_ref, v_ref, bias_ref, o_ref, *, rows, R):
    t = pl.program_id(2)

    def body(it, carry):
        st = []
        for j in range(NA_GROUP):
            i = it * NA_GROUP + j
            r = t * R + i
            r0 = jnp.clip(r - NA_KH // 2, 0, rows - NA_KH)
            st.append(dict(shift=r0 - r + (NA_KH - 1),
                           qs=pl.ds(pl.multiple_of(i * GRID_W, GRID_W), GRID_W),
                           ks=pl.ds(pl.multiple_of(r0 * GRID_W, GRID_W), NA_KEYS)))
        for s in st:
            s["s"] = _dot_nt(q_ref[0, s["qs"], :], k_ref[0, s["ks"], :]) * ATTN_SCALE + bias_ref[0, s["shift"]]
        for s in st:
            s["m"] = jnp.max(s["s"], axis=-1, keepdims=True)
            s["e"] = jnp.exp(s["s"] - s["m"])
            s["den"] = jnp.sum(s["e"], axis=-1, keepdims=True)
        for s in st:
            o_ref[0, s["qs"], :] = (_dot(s["e"].astype(BF16), v_ref[0, s["ks"], :]) / s["den"]).astype(o_ref.dtype)
        return carry
    lax.fori_loop(0, R // NA_GROUP, body, 0)


def _na_attention(qn, kn, vn, bias, R=16):
    B, L, _ = qn.shape
    rows = L // GRID_W
    assert rows >= NA_KH and rows % R == 0
    T = R * GRID_W
    return pl.pallas_call(
        functools.partial(_na_kernel, rows=rows, R=R),
        grid=(B, N_HEADS, rows // R),
        in_specs=[
            pl.BlockSpec((1, T, HEAD_DIM), lambda b, h, t: (b, t, h)),
            pl.BlockSpec((1, L, HEAD_DIM), lambda b, h, t: (b, 0, h)),
            pl.BlockSpec((1, L, HEAD_DIM), lambda b, h, t: (b, 0, h)),
            pl.BlockSpec((1, NA_KH, GRID_W, NA_KEYS), lambda b, h, t: (h, 0, 0, 0)),
        ],
        out_specs=pl.BlockSpec((1, T, HEAD_DIM), lambda b, h, t: (b, t, h)),
        out_shape=jax.ShapeDtypeStruct((B, L, GROUP_WIDTH), BF16),
        compiler_params=_cparams("parallel", "parallel", "arbitrary"),
        name="na_attention",
    )(qn, kn, vn, bias)


def _cast_kernel(x_ref, o_ref):
    o_ref[...] = x_ref[...].astype(o_ref.dtype)


def _cast_part(u, part, dtype, tl=1024):
    B, L, _ = u.shape
    return pl.pallas_call(
        _cast_kernel,
        grid=(B, L // tl),
        in_specs=[pl.BlockSpec((1, tl, GROUP_WIDTH), lambda b, t: (b, t, part))],
        out_specs=pl.BlockSpec((1, tl, GROUP_WIDTH), lambda b, t: (b, t, 0)),
        out_shape=jax.ShapeDtypeStruct((B, L, GROUP_WIDTH), dtype),
        compiler_params=_cparams("parallel", "parallel"),
        name="cast_part",
    )(u)


def _bidir_specs(T, n, width, part_f, part_b, heads_per_step):
    w = heads_per_step * width
    return (pl.BlockSpec((1, T, w), lambda b, hg, t: (b, t, part_f * (GROUP_WIDTH // w) + hg)),
            pl.BlockSpec((1, T, w), lambda b, hg, t: (b, n - 1 - t, part_b * (GROUP_WIDTH // w) + hg)))


def _hgrn_kernel(qf_ref, qb_ref, ff_ref, fb_ref, vf_ref, vb_ref, lg_ref, of_ref, ob_ref,
                 s_ref, s0_ref, kk_ref, b_ref, *, layer, T, HP):
    @pl.when(pl.program_id(2) == 0)
    def _():
        s_ref[...] = jnp.zeros_like(s_ref)

    n_chunks = T // CHUNK
    n_sub = CHUNK // SUB
    ci = _iota2((CHUNK, CHUNK), 1)

    chains = []
    for hh in range(HP):
        cols = slice(hh * HEAD_DIM, (hh + 1) * HEAD_DIM)
        for d in range(2):
            lg = lg_ref[:, d, 0, cols]
            ex = jnp.exp(lg - jnp.max(lg, axis=0, keepdims=True))
            p = ex / jnp.sum(ex, axis=0, keepdims=True)
            cs = p[0:1]
            for i in range(1, layer + 1):
                cs = cs + p[i:i + 1]
            fwd = d == 0
            chains.append(dict(
                idx=2 * hh + d, cols=cols, fwd=fwd, lb=cs - p[0:1],
                q=(qf_ref, qb_ref)[d], f=(ff_ref, fb_ref)[d], v=(vf_ref, vb_ref)[d], o=(of_ref, ob_ref)[d],
                tri=_tri4(fwd), incl=_order_masks(fwd, CHUNK)[0]))

    def exact_scores(qq, kk, b):
        kk_ref[...] = kk
        b_ref[...] = b

        def col(j, a):
            kj = kk_ref[pl.ds(j, 1), :]
            bj = b_ref[pl.ds(j, 1), :]
            tt = qq * kj * jnp.exp(jnp.minimum(b - bj, 0.0))
            return jnp.where(ci == j, jnp.sum(tt, axis=-1, keepdims=True), a)
        return lax.fori_loop(0, CHUNK, col, jnp.zeros((CHUNK, CHUNK), F32))

    def step_all(step, worst, exact):
        st = []
        for ch in chains:
            c = step if ch["fwd"] else n_chunks - 1 - step
            rows = pl.ds(pl.multiple_of(c * CHUNK, CHUNK), CHUNK)
            xf = ch["f"][0, rows, ch["cols"]]
            lb = ch["lb"]
            st.append(dict(
                rows=rows, v=ch["v"][0, rows, ch["cols"]].astype(BF16), qq=_silu(ch["q"][0, rows, ch["cols"]]),
                kk=(1.0 - lb) * _sigmoid(-xf),
                g=jnp.log(jnp.maximum(lb + (1.0 - lb) * _sigmoid(xf), LOG_FLOOR))))
        for ch, s in zip(chains, st):
            s["b"] = _dot(ch["tri"], _stack3(s["g"]))
        for ch, s in zip(chains, st):
            fwd, b = ch["fwd"], s["b"]
            bx = b - s["g"]
            s["tot"] = b[CHUNK - 1:CHUNK] if fwd else b[0:1]
            s["refs"] = []
            for i in range(n_sub):
                lo, hi = i * SUB, (i + 1) * SUB
                r_i = bx[lo:lo + 1] if fwd else bx[hi - 1:hi]
                end = b[hi - 1:hi] if fwd else b[lo:lo + 1]
                s["refs"].append(r_i)
                worst = jnp.maximum(worst, r_i - end)
        if exact:
            for s in st:
                s["a"] = exact_scores(s["qq"], s["kk"], s["b"])
        else:
            for s in st:
                s["blocks"] = []
            for i in range(n_sub):
                lo, hi = i * SUB, (i + 1) * SUB
                for s in st:
                    qi = s["qq"][lo:hi] * jnp.exp(s["b"][lo:hi] - s["refs"][i])
                    kt = s["kk"] * jnp.exp(jnp.minimum(s["refs"][i] - s["b"], EXP_CLAMP))
                    s["blocks"].append(_dot_nt(qi.astype(BF16), kt.astype(BF16)))
            for s in st:
                s["a"] = jnp.concatenate(s["blocks"], axis=0)
        for ch, s in zip(chains, st):
            s["st"] = s_ref[ch["idx"]]
            s["o"] = _dot_nt((s["qq"] * jnp.exp(s["b"])).astype(BF16), s["st"].astype(BF16))
        for ch, s in zip(chains, st):
            a = jnp.where(ch["incl"], s["a"], 0.0)
            ch["o"][0, s["rows"], ch["cols"]] = s["o"] + _dot(a.astype(BF16), s["v"])
        for ch, s in zip(chains, st):
            kdec = (s["kk"] * jnp.exp(s["tot"] - s["b"])).astype(BF16)
            s_ref[ch["idx"]] = jnp.exp(s["tot"]) * s["st"] + _dot_tn(s["v"], kdec)
        return worst

    def run(exact):
        return lax.fori_loop(0, n_chunks, functools.partial(step_all, exact=exact),
                             jnp.zeros((1, HEAD_DIM), F32))

    s0_ref[...] = s_ref[...]
    worst = run(False)

    @pl.when(jnp.max(worst) > EXP_CLAMP)
    def _():
        s_ref[...] = s0_ref[...]
        run(True)


def _hgrn_scan(u, lb_logits, layer, T=512, HP=4):
    B, L, _ = u.shape
    n = L // T
    w = HP * HEAD_DIM
    qf, qb = _bidir_specs(T, n, HEAD_DIM, P_HG_Q, P_HG_Q, HP)
    ff, fb = _bidir_specs(T, n, HEAD_DIM, P_HG_FF, P_HG_FB, HP)
    vf, vb = _bidir_specs(T, n, HEAD_DIM, P_HG_I, P_HG_I, HP)
    of, ob = _bidir_specs(T, n, HEAD_DIM, 0, 0, HP)
    oshape = jax.ShapeDtypeStruct((B, L, GROUP_WIDTH), F32)
    return pl.pallas_call(
        functools.partial(_hgrn_kernel, layer=layer, T=T, HP=HP),
        grid=(B, N_HEADS // HP, n),
        in_specs=[qf, qb, ff, fb, vf, vb,
                  pl.BlockSpec((DEPTH, 2, 1, w), lambda b, hg, t: (0, 0, 0, hg))],
        out_specs=[of, ob],
        out_shape=[oshape, oshape],
        scratch_shapes=[pltpu.VMEM((2 * HP, HEAD_DIM, HEAD_DIM), F32),
                        pltpu.VMEM((2 * HP, HEAD_DIM, HEAD_DIM), F32),
                        pltpu.VMEM((CHUNK, HEAD_DIM), F32),
                        pltpu.VMEM((CHUNK, HEAD_DIM), F32)],
        compiler_params=_cparams("parallel", "parallel", "arbitrary"),
        name="hgrn_scan",
    )(u, u, u, u, u, u, lb_logits.reshape(DEPTH, 2, 1, GROUP_WIDTH))


def _gated_norm_kernel(of_ref, ob_ref, z_ref, g_ref, y_ref):
    for h in range(N_HEADS):
        cols = slice(h * HEAD_DIM, (h + 1) * HEAD_DIM)
        o = of_ref[0, :, cols] + ob_ref[0, :, cols]
        y_ref[0, :, cols] = (_rms(o) * g_ref[...] * _silu(z_ref[0, :, cols])).astype(y_ref.dtype)


def _gated_norm(o_f, o_b, u, part_z, g, tl=512):
    B, L, _ = o_f.shape
    ospec = pl.BlockSpec((1, tl, GROUP_WIDTH), lambda b, t: (b, t, 0))
    return pl.pallas_call(
        _gated_norm_kernel,
        grid=(B, L // tl),
        in_specs=[
            ospec, ospec,
            pl.BlockSpec((1, tl, GROUP_WIDTH), lambda b, t: (b, t, part_z)),
            pl.BlockSpec((1, HEAD_DIM), lambda b, t: (0, 0)),
        ],
        out_specs=ospec,
        out_shape=jax.ShapeDtypeStruct((B, L, GROUP_WIDTH), BF16),
        compiler_params=_cparams("parallel", "parallel"),
        name="gated_norm",
    )(o_f, o_b, u, g.reshape(1, HEAD_DIM))


DN_HALO = 8


def _dn_prep_kernel(*refs, tl):
    w_ref = refs[9]
    outs = refs[10:13]
    t = pl.program_id(1)
    has_prev = t > 0
    has_next = t < pl.num_programs(1) - 1
    ext_rows = tl + 2 * DN_HALO
    for kind in range(3):
        prev_ref, cur_ref, next_ref = refs[3 * kind:3 * kind + 3]
        for h in range(N_HEADS):
            cols = slice(h * HEAD_DIM, (h + 1) * HEAD_DIM)
            wcols = slice(kind * GROUP_WIDTH + h * HEAD_DIM, kind * GROUP_WIDTH + (h + 1) * HEAD_DIM)
            ext = jnp.concatenate([
                jnp.where(has_prev, prev_ref[0, :, cols], 0.0),
                cur_ref[0, :, cols],
                jnp.where(has_next, next_ref[0, :, cols], 0.0)], axis=0)
            acc = None
            for j in range(DN_CONV):
                tap = pltpu.roll(ext, (DN_CONV // 2 - j) % ext_rows, 0)[DN_HALO:DN_HALO + tl]
                term = tap * w_ref[j:j + 1, wcols]
                acc = term if acc is None else acc + term
            y = _silu(acc)
            if kind < 2:
                y = y * lax.rsqrt(jnp.sum(y * y, axis=-1, keepdims=True) + NORM_EPS)
            if kind == 0:
                y = y * ATTN_SCALE
            outs[kind][0, :, cols] = y


def _dn_prep(u, conv_w, tl=512):
    B, L, _ = u.shape
    nh = tl // DN_HALO
    last = L // DN_HALO - 1
    in_specs = []
    for kind in range(3):
        part = P_DN_QKV + kind
        in_specs += [
            pl.BlockSpec((1, DN_HALO, GROUP_WIDTH),
                         lambda b, t, part=part: (b, jnp.maximum(t * nh - 1, 0), part)),
            pl.BlockSpec((1, tl, GROUP_WIDTH), lambda b, t, part=part: (b, t, part)),
            pl.BlockSpec((1, DN_HALO, GROUP_WIDTH),
                         lambda b, t, part=part: (b, jnp.minimum((t + 1) * nh, last), part)),
        ]
    in_specs.append(pl.BlockSpec((DN_CONV, 3 * GROUP_WIDTH), lambda b, t: (0, 0)))
    ospec = pl.BlockSpec((1, tl, GROUP_WIDTH), lambda b, t: (b, t, 0))
    return pl.pallas_call(
        functools.partial(_dn_prep_kernel, tl=tl),
        grid=(B, L // tl),
        in_specs=in_specs,
        out_specs=[ospec, ospec, ospec],
        out_shape=[jax.ShapeDtypeStruct((B, L, GROUP_WIDTH), F32)] * 3,
        compiler_params=_cparams("parallel", "parallel"),
        name="dn_prep",
    )(*([u] * 9), conv_w)


def _softplus(x):
    return jnp.maximum(x, 0.0) + jnp.log1p(jnp.exp(-jnp.abs(x)))


def _dn_gates_kernel(ab_ref, alog_ref, dtb_ref, cols_ref, rows_ref, *, T):
    lane = _iota2((1, HEAD_DIM), 1)
    neg_a = -jnp.exp(alog_ref[...])
    tri_f, tri_b, eye4 = _tri4(True), _tri4(False), _eye4()

    def body(c, carry):
        rows = pl.ds(pl.multiple_of(c * CHUNK, CHUNK), CHUNK)
        ab = ab_ref[0, rows, :]
        g3 = _stack3(neg_a * _softplus(ab + dtb_ref[...]))
        G = jnp.where(lane < N_HEADS, _dot(tri_f, g3), _dot(tri_b, g3))
        cols_ref[0, rows, :] = jnp.where(lane < 2 * N_HEADS, G, _sigmoid(ab))
        rows_ref[0, c] = _dot_tn(_stack3(G), eye4)[0:4 * N_HEADS]
        return carry
    lax.fori_loop(0, T // CHUNK, body, 0)


def _dn_gates(uab, a_log, dt_bias, T=1024):
    B, L, _ = uab.shape
    pad = HEAD_DIM - 2 * N_HEADS
    alog = jnp.pad(a_log.astype(F32).reshape(1, 2 * N_HEADS), ((0, 0), (0, pad)))
    dtb = jnp.pad(dt_bias.astype(F32).reshape(1, 2 * N_HEADS), ((0, 0), (0, pad)))
    pspec = pl.BlockSpec((1, HEAD_DIM), lambda b, t: (0, 0))
    return pl.pallas_call(
        functools.partial(_dn_gates_kernel, T=T),
        grid=(B, L // T),
        in_specs=[pl.BlockSpec((1, T, HEAD_DIM), lambda b, t: (b, t, 0)), pspec, pspec],
        out_specs=[pl.BlockSpec((1, T, HEAD_DIM), lambda b, t: (b, t, 0)),
                   pl.BlockSpec((1, T // CHUNK, 4 * N_HEADS, HEAD_DIM), lambda b, t: (b, t, 0, 0))],
        out_shape=[jax.ShapeDtypeStruct((B, L, HEAD_DIM), F32),
                   jax.ShapeDtypeStruct((B, L // CHUNK, 4 * N_HEADS, HEAD_DIM), F32)],
        compiler_params=_cparams("parallel", "parallel"),
        name="dn_gates",
    )(uab, alog, dtb)


def _lhs3(x):
    hi, lo = _split2(x)
    return jnp.concatenate([hi, lo], axis=1)


def _rhs3(x):
    hi, lo = _split2(x)
    return jnp.concatenate([hi, lo, hi, jnp.zeros_like(hi)], axis=0)


def _dn_kernel(qf_ref, qb_ref, kf_ref, kb_ref, vf_ref, vb_ref, gcf_ref, gcb_ref, grf_ref, grb_ref,
               of_ref, ob_ref, s_ref, *, T, HP):
    hg = pl.program_id(1)

    @pl.when(pl.program_id(2) == 0)
    def _():
        s_ref[...] = jnp.zeros_like(s_ref)

    n_chunks = T // CHUNK
    lane = _iota2((1, HEAD_DIM), 1)
    eye = jnp.where(_iota2((CHUNK, HEAD_DIM), 0) == (_iota2((CHUNK, HEAD_DIM), 1) & (CHUNK - 1)), 1.0, 0.0)

    chains = []
    for hh in range(HP):
        for d in range(2):
            fwd = d == 0
            incl, strict = _order_masks(fwd, HEAD_DIM)
            chains.append(dict(
                idx=2 * hh + d, cols=slice(hh * HEAD_DIM, (hh + 1) * HEAD_DIM), fwd=fwd,
                col=d * N_HEADS + hg * HP + hh, incl=incl, strict=strict,
                q=(qf_ref, qb_ref)[d], k=(kf_ref, kb_ref)[d], v=(vf_ref, vb_ref)[d],
                gc=(gcf_ref, gcb_ref)[d], gr=(grf_ref, grb_ref)[d], o=(of_ref, ob_ref)[d]))

    def body(step, carry):
        st = []
        for ch in chains:
            cols, col = ch["cols"], ch["col"]
            c = step if ch["fwd"] else n_chunks - 1 - step
            rows = pl.ds(pl.multiple_of(c * CHUNK, CHUNK), CHUNK)
            gates = ch["gc"][0, rows, :]
            k = ch["k"][0, rows, cols]
            kh = k.astype(BF16)
            st.append(dict(
                rows=rows, q=ch["q"][0, rows, cols], k=k, v=ch["v"][0, rows, cols],
                k2=jnp.concatenate([kh, kh], axis=0),
                g_col=jnp.sum(jnp.where(lane == col, gates, 0.0), axis=-1, keepdims=True),
                beta=jnp.sum(jnp.where(lane == col + 2 * N_HEADS, gates, 0.0), axis=-1, keepdims=True),
                g_row=ch["gr"][0, c, pl.ds(col, 1), :]))
        for ch, s in zip(chains, st):
            s["gamma"] = jnp.where(ch["incl"], jnp.exp(jnp.minimum(s["g_col"] - s["g_row"], 0.0)), 0.0)
            s["kb"] = s["k"] * s["beta"]
        for ch, s in zip(chains, st):
            n_mat = jnp.where(ch["strict"], _dot_nt(s["kb"].astype(BF16), s["k2"]) * s["gamma"], 0.0)
            s["x"] = eye - n_mat
            s["nl"], s["nr"] = _lhs3(n_mat), _rhs3(n_mat)

        order = 1
        while True:
            for s in st:
                s["nk"] = _dot(s["nl"], s["nr"])
            for s in st:
                s["nr"] = _rhs3(s["nk"])
                s["x"] = s["x"] + _dot(_lhs3(s["x"]), s["nr"])
            order *= 2
            if 2 * order >= CHUNK:
                break
            for s in st:
                s["nl"] = _lhs3(s["nk"])

        for s in st:
            s["eg"] = jnp.exp(s["g_col"])
            rhs = jnp.concatenate([s["kb"] * s["eg"], s["v"] * s["beta"]], axis=1)
            s["wu"] = _dot(_lhs3(s["x"]), _rhs3(rhs))
        for ch, s in zip(chains, st):
            s["a_qk"] = (_dot_nt(s["q"].astype(BF16), s["k2"]) * s["gamma"])[:, 0:CHUNK].astype(BF16)
            s["s"] = s_ref[ch["idx"]]
            s["sb"] = s["s"].astype(BF16)
        for s in st:
            s["v_new"] = (s["wu"][:, HEAD_DIM:] - _dot(s["wu"][:, :HEAD_DIM].astype(BF16), s["sb"])).astype(BF16)
        for ch, s in zip(chains, st):
            ch["o"][0, s["rows"], ch["cols"]] = (_dot((s["q"] * s["eg"]).astype(BF16), s["sb"])
                                                 + _dot(s["a_qk"], s["v_new"]))
        for ch, s in zip(chains, st):
            g_last = s["g_col"][CHUNK - 1:CHUNK] if ch["fwd"] else s["g_col"][0:1]
            k_dec = (s["k"] * jnp.exp(g_last - s["g_col"])).astype(BF16)
            s_ref[ch["idx"]] = jnp.exp(g_last) * s["s"] + _dot_tn(k_dec, s["v_new"])
        return carry
    lax.fori_loop(0, n_chunks, body, 0)


def _dn_scan(qh, kh, vh, gate_cols, gate_rows, T=512, HP=4):
    B, L, _ = qh.shape
    n = L // T
    hf, hb = _bidir_specs(T, n, HEAD_DIM, 0, 0, HP)
    gcf = pl.BlockSpec((1, T, HEAD_DIM), lambda b, hg, t: (b, t, 0))
    gcb = pl.BlockSpec((1, T, HEAD_DIM), lambda b, hg, t: (b, n - 1 - t, 0))
    rshape = (1, T // CHUNK, 4 * N_HEADS, HEAD_DIM)
    grf = pl.BlockSpec(rshape, lambda b, hg, t: (b, t, 0, 0))
    grb = pl.BlockSpec(rshape, lambda b, hg, t: (b, n - 1 - t, 0, 0))
    oshape = jax.ShapeDtypeStruct((B, L, GROUP_WIDTH), F32)
    return pl.pallas_call(
        functools.partial(_dn_kernel, T=T, HP=HP),
        grid=(B, N_HEADS // HP, n),
        in_specs=[hf, hb, hf, hb, hf, hb, gcf, gcb, grf, grb],
        out_specs=[hf, hb],
        out_shape=[oshape, oshape],
        scratch_shapes=[pltpu.VMEM((2 * HP, HEAD_DIM, HEAD_DIM), F32)],
        compiler_params=_cparams("parallel", "parallel", "arbitrary"),
        name="dn_scan",
    )(qh, qh, kh, kh, vh, vh, gate_cols, gate_cols, gate_rows, gate_rows)


def _encoder_layer(x, mod, layer, p, rope_tabs):
    u, uab = _in_proj(x, p["norm1_g"][layer], mod, p["w_in_main"][layer], p["w_in_ab"][layer])

    swa_q, swa_k = _qk_prep(u, P_SWA_Q, P_SWA_K, p["swa_q_norm"][layer], p["swa_k_norm"][layer],
                            rope_tabs, F32)
    y_swa = _swa_attention(swa_q, swa_k, u)

    o_hg = _hgrn_scan(u, p["hgrn_lb_logits"], layer)
    y_hg = _gated_norm(*o_hg, u, P_HG_G, p["hgrn_norm_g"][layer])

    na_q, na_k = _qk_prep(u, P_NA_Q, P_NA_K, p["na_q_norm"][layer], p["na_k_norm"][layer], None, BF16)
    na_v = _cast_part(u, P_NA_V, BF16)
    y_na = _na_attention(na_q, na_k, na_v, p["na_bias"][layer])

    dq, dk, dv = _dn_prep(u, p["dn_conv_w"][layer])
    gate_cols, gate_rows = _dn_gates(uab, p["dn_a_log"][layer], p["dn_dt_bias"][layer])
    o_dn = _dn_scan(dq, dk, dv, gate_cols, gate_rows)
    y_dn = _gated_norm(*o_dn, u, P_DN_Z, p["dn_norm_g"][layer])

    x = _out_proj(x, mod, (y_swa, y_hg, y_na, y_dn), p["w_out"][layer])
    return _mlp(x, p["norm2_g"][layer], mod, p["w_mlp_in"][layer], p["w_mlp_out"][layer])


def _run_trunk(x, mod_all, p):
    B, L, _ = x.shape
    rope_tabs = _rope_tables(L)
    for layer in range(DEPTH):
        x = _encoder_layer(x, mod_all[layer].reshape(B, 6, D_MODEL), layer, p, rope_tabs)
    return x


def kernel(x_prompt, x_sample, c_prompt, c_sample, norm1_g, norm2_g, ada_w, ada_b, w_in, w_out,
           swa_q_norm, swa_k_norm, hgrn_lb_logits, hgrn_norm_g, na_q_norm, na_k_norm, na_rpb,
           dn_conv_w, dn_a_log, dn_dt_bias, dn_norm_g, w_mlp_in, w_mlp_out):
    bp, bs = c_prompt.shape[0], c_sample.shape[0]
    pad_rows = -(bp + bs) % 8
    c_all = jnp.concatenate([c_prompt, c_sample, jnp.zeros((pad_rows, D_MODEL), F32)], axis=0)
    mod_all = _modulation(c_all, ada_w, ada_b)

    p = dict(
        norm1_g=norm1_g, norm2_g=norm2_g,
        w_in_main=w_in[:, :, :IN_MAIN].astype(BF16),
        w_in_ab=jnp.pad(w_in[:, :, IN_MAIN:], ((0, 0), (0, 0), (0, HEAD_DIM - 4 * N_HEADS))).astype(BF16),
        w_out=w_out.astype(BF16),
        swa_q_norm=swa_q_norm, swa_k_norm=swa_k_norm,
        hgrn_lb_logits=hgrn_lb_logits.astype(F32), hgrn_norm_g=hgrn_norm_g,
        na_q_norm=na_q_norm, na_k_norm=na_k_norm,
        na_bias=jnp.stack([_na_bias_table(na_rpb[l]) for l in range(DEPTH)]),
        dn_conv_w=dn_conv_w, dn_a_log=dn_a_log, dn_dt_bias=dn_dt_bias, dn_norm_g=dn_norm_g,
        w_mlp_in=w_mlp_in.astype(BF16), w_mlp_out=w_mlp_out.astype(BF16),
    )
    y_prompt = _run_trunk(x_prompt, mod_all[:, :bp], p)
    y_sample = _run_trunk(x_sample, mod_all[:, bp:bp + bs], p)
    return (y_prompt, y_sample)
```

```python
import functools
import math

import jax
import jax.numpy as jnp
from jax import lax
from jax.experimental import pallas as pl
from jax.experimental.pallas import tpu as pltpu

F32 = jnp.float32
BF16 = jnp.bfloat16

D_MODEL = 2048
DEPTH = 4
HEAD_DIM = 128
N_HEADS = 4
GROUP_WIDTH = N_HEADS * HEAD_DIM
IN_MAIN = 15 * GROUP_WIDTH
D_FF = 4 * D_MODEL
NORM_EPS = 1e-6
ROPE_THETA = 500000.0
ROPE_DIM = HEAD_DIM // 4
DILATED_CONFIGS = ((128, 1), (512, 4), (2048, 16))
GRID_W = 64
NA_KH = 8
NA_KW = 16
CHUNK = 64
SUB = 16
DN_CONV = 5
MASK_VALUE = -1e30
LOG_FLOOR = 1e-30
ATTN_SCALE = HEAD_DIM ** -0.5
EXP_CLAMP = 60.0
HIGHEST = lax.Precision.HIGHEST

P_SWA_Q, P_SWA_K, P_SWA_V = 0, 1, 2
P_HG_Q, P_HG_FF, P_HG_FB, P_HG_I, P_HG_G = 3, 4, 5, 6, 7
P_NA_Q, P_NA_K, P_NA_V = 8, 9, 10
P_DN_QKV, P_DN_Z = 11, 14

VMEM_LIMIT = 56 * 1024 * 1024


def _cparams(*sem):
    return pltpu.CompilerParams(dimension_semantics=sem, vmem_limit_bytes=VMEM_LIMIT)


def _dot(a, b):
    return jnp.dot(a, b, preferred_element_type=F32)


def _dot_nt(a, b):
    return lax.dot_general(a, b, (((1,), (1,)), ((), ())), preferred_element_type=F32)


def _dot_tn(a, b):
    return lax.dot_general(a, b, (((0,), (0,)), ((), ())), preferred_element_type=F32)


def _sigmoid(x):
    return jax.nn.sigmoid(x)


def _silu(x):
    return x * _sigmoid(x)


def _rms(x):
    return x * lax.rsqrt(jnp.mean(x * x, axis=-1, keepdims=True) + NORM_EPS)


def _split2(x):
    hi = x.astype(BF16)
    return hi, (x - hi.astype(F32)).astype(BF16)


def _split3(x):
    hi = x.astype(BF16)
    r1 = x - hi.astype(F32)
    mid = r1.astype(BF16)
    return hi, mid, (r1 - mid.astype(F32)).astype(BF16)


def _stack3(x):
    hi, mid, lo = _split3(x)
    return jnp.concatenate([hi, mid, lo, jnp.zeros_like(hi)], axis=0)


def _iota2(shape, axis):
    return lax.broadcasted_iota(jnp.int32, shape, axis)


def _tri4(fwd):
    ri, ci = _iota2((CHUNK, 4 * CHUNK), 0), _iota2((CHUNK, 4 * CHUNK), 1)
    cm = ci & (CHUNK - 1)
    order = (cm <= ri) if fwd else (cm >= ri)
    return jnp.where(order & (ci < 3 * CHUNK), 1.0, 0.0).astype(BF16)


def _eye4():
    ri, ci = _iota2((4 * CHUNK, 2 * CHUNK), 0), _iota2((4 * CHUNK, 2 * CHUNK), 1)
    same = (ri & (CHUNK - 1)) == (ci & (CHUNK - 1))
    return jnp.where(same & (ri < 3 * CHUNK), 1.0, 0.0).astype(BF16)


def _order_masks(fwd, width):
    ri, ci = _iota2((CHUNK, width), 0), _iota2((CHUNK, width), 1) & (CHUNK - 1)
    if fwd:
        return ci <= ri, ci < ri
    return ci >= ri, ci > ri


def _mod_kernel(c_ref, w_ref, b_ref, o_ref):
    a = _silu(c_ref[...]).astype(BF16)
    o_ref[0] = _dot(a, w_ref[0].astype(BF16)) + b_ref[0]


def _modulation(c_all, ada_w, ada_b):
    rows, d = c_all.shape
    n = ada_w.shape[-1]
    tn = 1024
    return pl.pallas_call(
        _mod_kernel,
        grid=(DEPTH, n // tn),
        in_specs=[
            pl.BlockSpec((rows, d), lambda l, j: (0, 0)),
            pl.BlockSpec((1, d, tn), lambda l, j: (l, 0, j)),
            pl.BlockSpec((1, 1, tn), lambda l, j: (l, 0, j)),
        ],
        out_specs=pl.BlockSpec((1, rows, tn), lambda l, j: (l, 0, j)),
        out_shape=jax.ShapeDtypeStruct((DEPTH, rows, n), F32),
        compiler_params=_cparams("parallel", "parallel"),
        name="adaln_mod",
    )(c_all, ada_w, ada_b.reshape(DEPTH, 1, n))


NORM_ROWS = 256


def _norm_mod_store(h_ref, x_ref, g, scale, shift, tm):
    def body(c, carry):
        rows = pl.ds(pl.multiple_of(c * NORM_ROWS, NORM_ROWS), NORM_ROWS)
        y = _rms(x_ref[0, rows, :]) * g
        h_ref[rows, :] = (y * (1.0 + scale) + shift).astype(BF16)
        return carry
    lax.fori_loop(0, tm // NORM_ROWS, body, 0)


def _in_proj_kernel(x_ref, g_ref, mod_ref, w_ref, wab_ref, u_ref, uab_ref, h_ref, *, tm):
    @pl.when(pl.program_id(2) == 0)
    def _():
        _norm_mod_store(h_ref, x_ref, g_ref[...], mod_ref[0, 1:2, :], mod_ref[0, 0:1, :], tm)
        uab_ref[0] = _dot(h_ref[...], wab_ref[...])
    u_ref[0] = _dot(h_ref[...], w_ref[...])


def _in_proj(x, g, mod, w_main, w_ab, layer, tm=512, tn=1536):
    B, L, D = x.shape
    N = IN_MAIN
    return pl.pallas_call(
        functools.partial(_in_proj_kernel, tm=tm),
        grid=(B, L // tm, N // tn),
        in_specs=[
            pl.BlockSpec((1, tm, D), lambda b, i, j: (b, i, 0)),
            pl.BlockSpec((1, D), lambda b, i, j: (0, 0)),
            pl.BlockSpec((1, 6, D), lambda b, i, j: (b, 0, 0)),
            pl.BlockSpec((None, D, tn), lambda b, i, j: (layer, 0, j)),
            pl.BlockSpec((D, HEAD_DIM), lambda b, i, j: (0, 0)),
        ],
        out_specs=[
            pl.BlockSpec((1, tm, tn), lambda b, i, j: (b, i, j)),
            pl.BlockSpec((1, tm, HEAD_DIM), lambda b, i, j: (b, i, 0)),
        ],
        out_shape=[jax.ShapeDtypeStruct((B, L, N), F32),
                   jax.ShapeDtypeStruct((B, L, HEAD_DIM), F32)],
        scratch_shapes=[pltpu.VMEM((tm, D), BF16)],
        compiler_params=_cparams("parallel", "parallel", "arbitrary"),
        name="in_proj",
    )(x, g.reshape(1, D), mod, w_main, w_ab)


def _gated_heads(of_ref, ob_ref, z_ref, g_ref):
    parts = []
    for h in range(N_HEADS):
        cols = slice(h * HEAD_DIM, (h + 1) * HEAD_DIM)
        o = of_ref[0, :, cols] + ob_ref[0, :, cols]
        parts.append((_rms(o) * g_ref[...] * _silu(z_ref[0, :, cols])).astype(BF16))
    return jnp.concatenate(parts, axis=1)


def _out_proj_kernel(x_ref, mod_ref, swa_ref, na_ref, hgf_ref, hgb_ref, hgz_ref, hgg_ref,
                     dnf_ref, dnb_ref, dnz_ref, dng_ref, w_ref, o_ref):
    ys = (swa_ref[0], _gated_heads(hgf_ref, hgb_ref, hgz_ref, hgg_ref),
          na_ref[0], _gated_heads(dnf_ref, dnb_ref, dnz_ref, dng_ref))
    acc = None
    for m, y in enumerate(ys):
        part = _dot(y, w_ref[m * GROUP_WIDTH:(m + 1) * GROUP_WIDTH, :])
        acc = part if acc is None else acc + part
    o_ref[0] = x_ref[0] + mod_ref[0, 2:3, :] * acc


def _out_proj(x, mod, u, y_swa, y_na, o_hg, g_hg, o_dn, g_dn, w_out, layer, tm=512):
    B, L, D = x.shape
    yspec = pl.BlockSpec((1, tm, GROUP_WIDTH), lambda b, i: (b, i, 0))
    gspec = pl.BlockSpec((1, HEAD_DIM), lambda b, i: (0, 0))
    zspec = lambda part: pl.BlockSpec((1, tm, GROUP_WIDTH), lambda b, i: (b, i, part))
    return pl.pallas_call(
        _out_proj_kernel,
        grid=(B, L // tm),
        in_specs=[
            pl.BlockSpec((1, tm, D), lambda b, i: (b, i, 0)),
            pl.BlockSpec((1, 6, D), lambda b, i: (b, 0, 0)),
            yspec, yspec,
            yspec, yspec, zspec(P_HG_G), gspec,
            yspec, yspec, zspec(P_DN_Z), gspec,
            pl.BlockSpec((None, D, D), lambda b, i: (layer, 0, 0)),
        ],
        out_specs=pl.BlockSpec((1, tm, D), lambda b, i: (b, i, 0)),
        out_shape=jax.ShapeDtypeStruct((B, L, D), F32),
        compiler_params=_cparams("parallel", "parallel"),
        name="out_proj",
    )(x, mod, y_swa, y_na, *o_hg, u, g_hg.reshape(1, HEAD_DIM), *o_dn, u, g_dn.reshape(1, HEAD_DIM), w_out)


def _mlp_kernel(x_ref, g_ref, mod_ref, w1_ref, w2_ref, o_ref, h_ref, *, tm):
    j = pl.program_id(2)

    @pl.when(j == 0)
    def _():
        _norm_mod_store(h_ref, x_ref, g_ref[...], mod_ref[0, 4:5, :], mod_ref[0, 3:4, :], tm)
        o_ref[...] = jnp.zeros_like(o_ref)

    half = w1_ref.shape[1] // 2
    hids = [jnp.square(jnp.maximum(_dot(h_ref[...], w1_ref[:, c * half:(c + 1) * half]), 0.0)).astype(BF16)
            for c in range(2)]
    o_ref[0] += _dot(hids[0], w2_ref[0:half, :]) + _dot(hids[1], w2_ref[half:, :])

    @pl.when(j == pl.num_programs(2) - 1)
    def _():
        o_ref[0] = x_ref[0] + mod_ref[0, 5:6, :] * o_ref[0]


def _mlp(x, g, mod, w1, w2, layer, tm=512, tf=1024):
    B, L, D = x.shape
    F = w1.shape[-1]
    return pl.pallas_call(
        functools.partial(_mlp_kernel, tm=tm),
        grid=(B, L // tm, F // tf),
        in_specs=[
            pl.BlockSpec((1, tm, D), lambda b, i, j: (b, i, 0)),
            pl.BlockSpec((1, D), lambda b, i, j: (0, 0)),
            pl.BlockSpec((1, 6, D), lambda b, i, j: (b, 0, 0)),
            pl.BlockSpec((None, D, tf), lambda b, i, j: (layer, 0, j)),
            pl.BlockSpec((None, tf, D), lambda b, i, j: (layer, j, 0)),
        ],
        out_specs=pl.BlockSpec((1, tm, D), lambda b, i, j: (b, i, 0)),
        out_shape=jax.ShapeDtypeStruct((B, L, D), F32),
        scratch_shapes=[pltpu.VMEM((tm, D), BF16)],
        compiler_params=_cparams("parallel", "parallel", "arbitrary"),
        name="mlp",
    )(x, g.reshape(1, D), mod, w1, w2)


def _qk_prep_kernel(*refs, rope):
    if rope:
        q_ref, k_ref, gq_ref, gk_ref, c_ref, s1_ref, s2_ref, qo_ref, ko_ref = refs
    else:
        q_ref, k_ref, v_ref, gq_ref, gk_ref, qo_ref, ko_ref, vo_ref = refs
        vo_ref[...] = v_ref[...].astype(vo_ref.dtype)
    for src, g_ref, dst in ((q_ref, gq_ref, qo_ref), (k_ref, gk_ref, ko_ref)):
        for h in range(N_HEADS):
            cols = slice(h * HEAD_DIM, (h + 1) * HEAD_DIM)
            y = _rms(src[0, :, cols]) * g_ref[...]
            if rope:
                y = (y * c_ref[...] + pltpu.roll(y, ROPE_DIM // 2, 1) * s1_ref[...]
                     + pltpu.roll(y, HEAD_DIM - ROPE_DIM // 2, 1) * s2_ref[...])
            dst[0, :, cols] = y.astype(dst.dtype)


def _rope_tables(L):
    half = ROPE_DIM // 2
    inv_freq = jnp.power(ROPE_THETA, -jnp.arange(half, dtype=F32) / half)
    ang = jnp.arange(L).astype(F32)[:, None] * inv_freq[None, :]
    cos, sin = jnp.cos(ang), jnp.sin(ang)
    zeros = jnp.zeros((L, HEAD_DIM - ROPE_DIM), F32)
    zh = jnp.zeros((L, half), F32)
    c = jnp.concatenate([cos, cos, jnp.ones((L, HEAD_DIM - ROPE_DIM), F32)], axis=1)
    s1 = jnp.concatenate([zh, sin, zeros], axis=1)
    s2 = jnp.concatenate([-sin, zh, zeros], axis=1)
    return c, s1, s2


def _qk_prep(u, part_q, part_k, part_v, gq, gk, rope_tabs, out_dtype, tl=512):
    B, L, _ = u.shape
    rope = rope_tabs is not None
    uspec = lambda part: pl.BlockSpec((1, tl, GROUP_WIDTH), lambda b, t, part=part: (b, t, part))
    gspec = pl.BlockSpec((1, HEAD_DIM), lambda b, t: (0, 0))
    tspec = pl.BlockSpec((tl, HEAD_DIM), lambda b, t: (t, 0))
    ospec = pl.BlockSpec((1, tl, GROUP_WIDTH), lambda b, t: (b, t, 0))
    gains = [gq.reshape(1, HEAD_DIM), gk.reshape(1, HEAD_DIM)]
    if rope:
        in_specs = [uspec(part_q), uspec(part_k), gspec, gspec, tspec, tspec, tspec]
        args = [u, u] + gains + list(rope_tabs)
    else:
        in_specs = [uspec(part_q), uspec(part_k), uspec(part_v), gspec, gspec]
        args = [u, u, u] + gains
    n_out = 2 if rope else 3
    return pl.pallas_call(
        functools.partial(_qk_prep_kernel, rope=rope),
        grid=(B, L // tl),
        in_specs=in_specs,
        out_specs=[ospec] * n_out,
        out_shape=[jax.ShapeDtypeStruct((B, L, GROUP_WIDTH), out_dtype)] * n_out,
        compiler_params=_cparams("parallel", "parallel"),
        name="qk_prep_rope" if rope else "qkv_prep",
    )(*args)


SWA_Q = 128
SWA_K = 2 * SWA_Q
SWA_RADIUS = 64
SWA_ROWS = 256
SWA_GROUP = 4


def _swa_kernel(q_ref, k_ref, v_ref, o_ref, ob_ref, ls_ref, mask_ref, *, L, T):
    t0 = pl.program_id(2) * T
    diff = _iota2((SWA_Q, SWA_K), 1) - _iota2((SWA_Q, SWA_K), 0)
    for i in range(3):
        mask_ref[i] = jnp.where(jnp.abs(diff - i * SWA_RADIUS) <= SWA_RADIUS, 0.0, MASK_VALUE)

    for bi, (window, dil) in enumerate(DILATED_CONFIGS):
        assert window // (2 * dil) == SWA_RADIUS
        span = SWA_Q * dil
        n_sub = T // span

        def body(it, carry, dil=dil, bi=bi, span=span, n_sub=n_sub):
            st = []
            for j in range(SWA_GROUP):
                idx = it * SWA_GROUP + j
                r, loc = idx // n_sub, (idx % n_sub) * span
                base = jnp.clip(t0 + loc - SWA_RADIUS * dil, 0, L - SWA_K * dil)
                if dil == 1:
                    qs = pl.ds(pl.multiple_of(loc, SWA_Q), SWA_Q)
                    ks = pl.ds(pl.multiple_of(base, SWA_RADIUS), SWA_K)
                else:
                    qs = pl.ds(loc + r, SWA_Q, stride=dil)
                    ks = pl.ds(base + r, SWA_K, stride=dil)
                st.append(dict(qs=qs, ks=ks, mask=(t0 + loc - base) // (dil * SWA_RADIUS)))
            for s in st:
                sc = _dot_nt(q_ref[0, s["qs"], :].astype(BF16), k_ref[0, s["ks"], :].astype(BF16))
                s["s"] = sc * ATTN_SCALE + mask_ref[s["mask"]]
            for s in st:
                s["m"] = jnp.max(s["s"], axis=-1, keepdims=True)
                s["e"] = jnp.exp(s["s"] - s["m"])
                s["den"] = jnp.sum(s["e"], axis=-1, keepdims=True)
            for s in st:
                pv = _dot(s["e"].astype(BF16), v_ref[0, s["ks"], :].astype(BF16))
                ob_ref[bi, s["qs"], :] = pv / s["den"]
                ls_ref[bi, s["qs"], :] = jnp.broadcast_to(s["m"] + jnp.log(s["den"]), (SWA_Q, HEAD_DIM))
            return carry
        lax.fori_loop(0, T // SWA_Q // SWA_GROUP, body, 0)

    def combine(c, carry):
        rows = pl.ds(pl.multiple_of(c * SWA_ROWS, SWA_ROWS), SWA_ROWS)
        l0, l1, l2 = ls_ref[0, rows, :], ls_ref[1, rows, :], ls_ref[2, rows, :]
        mx = jnp.maximum(l0, jnp.maximum(l1, l2))
        w0, w1, w2 = jnp.exp(l0 - mx), jnp.exp(l1 - mx), jnp.exp(l2 - mx)
        num = w0 * ob_ref[0, rows, :] + w1 * ob_ref[1, rows, :] + w2 * ob_ref[2, rows, :]
        o_ref[0, rows, :] = (num / (w0 + w1 + w2)).astype(o_ref.dtype)
        return carry
    lax.fori_loop(0, T // SWA_ROWS, combine, 0)


def _swa_attention(qn, kn, u, T=2048):
    B, L, _ = qn.shape
    max_dil = max(d for _, d in DILATED_CONFIGS)
    assert L % T == 0 and T % (SWA_Q * max_dil) == 0 and L >= SWA_K * max_dil
    nb = len(DILATED_CONFIGS)
    return pl.pallas_call(
        functools.partial(_swa_kernel, L=L, T=T),
        grid=(B, N_HEADS, L // T),
        in_specs=[
            pl.BlockSpec((1, T, HEAD_DIM), lambda b, h, t: (b, t, h)),
            pl.BlockSpec((1, L, HEAD_DIM), lambda b, h, t: (b, 0, h)),
            pl.BlockSpec((1, L, HEAD_DIM), lambda b, h, t: (b, 0, P_SWA_V * N_HEADS + h)),
        ],
        out_specs=pl.BlockSpec((1, T, HEAD_DIM), lambda b, h, t: (b, t, h)),
        out_shape=jax.ShapeDtypeStruct((B, L, GROUP_WIDTH), BF16),
        scratch_shapes=[pltpu.VMEM((nb, T, HEAD_DIM), F32), pltpu.VMEM((nb, T, HEAD_DIM), F32),
                        pltpu.VMEM((3, SWA_Q, SWA_K), F32)],
        compiler_params=_cparams("parallel", "parallel", "arbitrary"),
        name="swa_attention",
    )(qn, kn, u)


NA_KEYS = NA_KH * GRID_W
NA_GROUP = 8


def _na_bias_table(rpb):
    cq = jnp.arange(GRID_W)
    ck = jnp.arange(GRID_W)
    c0 = jnp.clip(cq - NA_KW // 2, 0, GRID_W - NA_KW)
    col_ok = (ck[None, :] >= c0[:, None]) & (ck[None, :] < c0[:, None] + NA_KW)
    col_off = jnp.clip(ck[None, :] - cq[:, None], -(NA_KW - 1), NA_KW - 1) + (NA_KW - 1)
    row_off = jnp.arange(NA_KH)[:, None] + jnp.arange(NA_KH)[None, :]
    bias = rpb.astype(F32)[:, row_off]
    bias = bias[..., col_off]
    bias = bias.transpose(0, 1, 3, 2, 4)
    bias = jnp.where(col_ok[None, None, :, None, :], bias, MASK_VALUE)
    return bias.reshape(rpb.shape[0], NA_KH, GRID_W, NA_KEYS)


def _na_kernel(q_ref, k_ref, v_ref, bias_ref, o_ref, *, rows, R):
    t = pl.program_id(2)

    def body(it, carry):
        st = []
        for j in range(NA_GROUP):
            i = it * NA_GROUP + j
            r = t * R + i
            r0 = jnp.clip(r - NA_KH // 2, 0, rows - NA_KH)
            st.append(dict(shift=r0 - r + (NA_KH - 1),
                           qs=pl.ds(pl.multiple_of(i * GRID_W, GRID_W), GRID_W),
                           ks=pl.ds(pl.multiple_of(r0 * GRID_W, GRID_W), NA_KEYS)))
        for s in st:
            s["s"] = _dot_nt(q_ref[0, s["qs"], :], k_ref[0, s["ks"], :]) * ATTN_SCALE + bias_ref[0, s["shift"]]
        for s in st:
            s["m"] = jnp.max(s["s"], axis=-1, keepdims=True)
            s["e"] = jnp.exp(s["s"] - s["m"])
            s["den"] = jnp.sum(s["e"], axis=-1, keepdims=True)
        for s in st:
            o_ref[0, s["qs"], :] = (_dot(s["e"].astype(BF16), v_ref[0, s["ks"], :]) / s["den"]).astype(o_ref.dtype)
        return carry
    lax.fori_loop(0, R // NA_GROUP, body, 0)


def _na_attention(qn, kn, vn, bias, R=16):
    B, L, _ = qn.shape
    rows = L // GRID_W
    assert rows >= NA_KH and rows % R == 0
    T = R * GRID_W
    return pl.pallas_call(
        functools.partial(_na_kernel, rows=rows, R=R),
        grid=(B, N_HEADS, rows // R),
        in_specs=[
            pl.BlockSpec((1, T, HEAD_DIM), lambda b, h, t: (b, t, h)),
            pl.BlockSpec((1, L, HEAD_DIM), lambda b, h, t: (b, 0, h)),
            pl.BlockSpec((1, L, HEAD_DIM), lambda b, h, t: (b, 0, h)),
            pl.BlockSpec((1, NA_KH, GRID_W, NA_KEYS), lambda b, h, t: (h, 0, 0, 0)),
        ],
        out_specs=pl.BlockSpec((1, T, HEAD_DIM), lambda b, h, t: (b, t, h)),
        out_shape=jax.ShapeDtypeStruct((B, L, GROUP_WIDTH), BF16),
        compiler_params=_cparams("parallel", "parallel", "arbitrary"),
        name="na_attention",
    )(qn, kn, vn, bias)


def _bidir_specs(T, n, width, part_f, part_b, heads_per_step):
    w = heads_per_step * width
    return (pl.BlockSpec((1, T, w), lambda b, hg, t: (b, t, part_f * (GROUP_WIDTH // w) + hg)),
            pl.BlockSpec((1, T, w), lambda b, hg, t: (b, n - 1 - t, part_b * (GROUP_WIDTH // w) + hg)))


def _hgrn_kernel(qf_ref, qb_ref, ff_ref, fb_ref, vf_ref, vb_ref, lg_ref, of_ref, ob_ref,
                 s_ref, s0_ref, kk_ref, b_ref, *, layer, T, HP):
    @pl.when(pl.program_id(2) == 0)
    def _():
        s_ref[...] = jnp.zeros_like(s_ref)

    n_chunks = T // CHUNK
    n_sub = CHUNK // SUB
    ci = _iota2((CHUNK, CHUNK), 1)

    chains = []
    for hh in range(HP):
        cols = slice(hh * HEAD_DIM, (hh + 1) * HEAD_DIM)
        for d in range(2):
            lg = lg_ref[:, d, 0, cols]
            ex = jnp.exp(lg - jnp.max(lg, axis=0, keepdims=True))
            p = ex / jnp.sum(ex, axis=0, keepdims=True)
            cs = p[0:1]
            for i in range(1, layer + 1):
                cs = cs + p[i:i + 1]
            fwd = d == 0
            chains.append(dict(
                idx=2 * hh + d, cols=cols, fwd=fwd, lb=cs - p[0:1],
                q=(qf_ref, qb_ref)[d], f=(ff_ref, fb_ref)[d], v=(vf_ref, vb_ref)[d], o=(of_ref, ob_ref)[d],
                tri=_tri4(fwd), incl=_order_masks(fwd, CHUNK)[0]))

    def exact_scores(qq, kk, b):
        kk_ref[...] = kk
        b_ref[...] = b

        def col(j, a):
            kj = kk_ref[pl.ds(j, 1), :]
            bj = b_ref[pl.ds(j, 1), :]
            tt = qq * kj * jnp.exp(jnp.minimum(b - bj, 0.0))
            return jnp.where(ci == j, jnp.sum(tt, axis=-1, keepdims=True), a)
        return lax.fori_loop(0, CHUNK, col, jnp.zeros((CHUNK, CHUNK), F32))

    def step_all(step, worst, exact):
        st = []
        for ch in chains:
            c = step if ch["fwd"] else n_chunks - 1 - step
            rows = pl.ds(pl.multiple_of(c * CHUNK, CHUNK), CHUNK)
            xf = ch["f"][0, rows, ch["cols"]]
            lb = ch["lb"]
            st.append(dict(
                rows=rows, v=ch["v"][0, rows, ch["cols"]].astype(BF16), qq=_silu(ch["q"][0, rows, ch["cols"]]),
                kk=(1.0 - lb) * _sigmoid(-xf),
                g=jnp.log(jnp.maximum(lb + (1.0 - lb) * _sigmoid(xf), LOG_FLOOR))))
        for ch, s in zip(chains, st):
            s["b"] = _dot(ch["tri"], _stack3(s["g"]))
        for ch, s in zip(chains, st):
            fwd, b = ch["fwd"], s["b"]
            bx = b - s["g"]
            s["tot"] = b[CHUNK - 1:CHUNK] if fwd else b[0:1]
            s["refs"] = []
            for i in range(n_sub):
                lo, hi = i * SUB, (i + 1) * SUB
                r_i = bx[lo:lo + 1] if fwd else bx[hi - 1:hi]
                end = b[hi - 1:hi] if fwd else b[lo:lo + 1]
                s["refs"].append(r_i)
                worst = jnp.maximum(worst, r_i - end)
        if exact:
            for s in st:
                s["a"] = exact_scores(s["qq"], s["kk"], s["b"])
        else:
            for s in st:
                s["blocks"] = []
            for i in range(n_sub):
                lo, hi = i * SUB, (i + 1) * SUB
                for s in st:
                    qi = s["qq"][lo:hi] * jnp.exp(s["b"][lo:hi] - s["refs"][i])
                    kt = s["kk"] * jnp.exp(jnp.minimum(s["refs"][i] - s["b"], EXP_CLAMP))
                    s["blocks"].append(_dot_nt(qi.astype(BF16), kt.astype(BF16)))
            for s in st:
                s["a"] = jnp.concatenate(s["blocks"], axis=0)
        for ch, s in zip(chains, st):
            s["st"] = s_ref[ch["idx"]]
            s["o"] = _dot_nt((s["qq"] * jnp.exp(s["b"])).astype(BF16), s["st"].astype(BF16))
        for ch, s in zip(chains, st):
            a = jnp.where(ch["incl"], s["a"], 0.0)
            ch["o"][0, s["rows"], ch["cols"]] = s["o"] + _dot(a.astype(BF16), s["v"])
        for ch, s in zip(chains, st):
            kdec = (s["kk"] * jnp.exp(s["tot"] - s["b"])).astype(BF16)
            s_ref[ch["idx"]] = jnp.exp(s["tot"]) * s["st"] + _dot_tn(s["v"], kdec)
        return worst

    def run(exact):
        return lax.fori_loop(0, n_chunks, functools.partial(step_all, exact=exact),
                             jnp.zeros((1, HEAD_DIM), F32))

    s0_ref[...] = s_ref[...]
    worst = run(False)

    @pl.when(jnp.max(worst) > EXP_CLAMP)
    def _():
        s_ref[...] = s0_ref[...]
        run(True)


def _hgrn_scan(u, lb_logits, layer, T=512, HP=4):
    B, L, _ = u.shape
    n = L // T
    w = HP * HEAD_DIM
    qf, qb = _bidir_specs(T, n, HEAD_DIM, P_HG_Q, P_HG_Q, HP)
    ff, fb = _bidir_specs(T, n, HEAD_DIM, P_HG_FF, P_HG_FB, HP)
    vf, vb = _bidir_specs(T, n, HEAD_DIM, P_HG_I, P_HG_I, HP)
    of, ob = _bidir_specs(T, n, HEAD_DIM, 0, 0, HP)
    oshape = jax.ShapeDtypeStruct((B, L, GROUP_WIDTH), F32)
    return pl.pallas_call(
        functools.partial(_hgrn_kernel, layer=layer, T=T, HP=HP),
        grid=(B, N_HEADS // HP, n),
        in_specs=[qf, qb, ff, fb, vf, vb,
                  pl.BlockSpec((DEPTH, 2, 1, w), lambda b, hg, t: (0, 0, 0, hg))],
        out_specs=[of, ob],
        out_shape=[oshape, oshape],
        scratch_shapes=[pltpu.VMEM((2 * HP, HEAD_DIM, HEAD_DIM), F32),
                        pltpu.VMEM((2 * HP, HEAD_DIM, HEAD_DIM), F32),
                        pltpu.VMEM((CHUNK, HEAD_DIM), F32),
                        pltpu.VMEM((CHUNK, HEAD_DIM), F32)],
        compiler_params=_cparams("parallel", "parallel", "arbitrary"),
        name="hgrn_scan",
    )(u, u, u, u, u, u, lb_logits.reshape(DEPTH, 2, 1, GROUP_WIDTH))


DN_HALO = 8
DN_GATE_GROUP = 4


def _dn_prep_kernel(*refs, tl):
    w_ref = refs[9]
    outs = refs[10:13]
    t = pl.program_id(1)
    has_prev = t > 0
    has_next = t < pl.num_programs(1) - 1
    ext_rows = tl + 2 * DN_HALO
    for kind in range(3):
        prev_ref, cur_ref, next_ref = refs[3 * kind:3 * kind + 3]
        for h in range(N_HEADS):
            cols = slice(h * HEAD_DIM, (h + 1) * HEAD_DIM)
            wcols = slice(kind * GROUP_WIDTH + h * HEAD_DIM, kind * GROUP_WIDTH + (h + 1) * HEAD_DIM)
            ext = jnp.concatenate([
                jnp.where(has_prev, prev_ref[0, :, cols], 0.0),
                cur_ref[0, :, cols],
                jnp.where(has_next, next_ref[0, :, cols], 0.0)], axis=0)
            acc = None
            for j in range(DN_CONV):
                tap = pltpu.roll(ext, (DN_CONV // 2 - j) % ext_rows, 0)[DN_HALO:DN_HALO + tl]
                term = tap * w_ref[j:j + 1, wcols]
                acc = term if acc is None else acc + term
            y = _silu(acc)
            if kind < 2:
                y = y * lax.rsqrt(jnp.sum(y * y, axis=-1, keepdims=True) + NORM_EPS)
            if kind == 0:
                y = y * ATTN_SCALE
            outs[kind][0, :, cols] = y


def _dn_prep(u, conv_w, tl=512):
    B, L, _ = u.shape
    nh = tl // DN_HALO
    last = L // DN_HALO - 1
    in_specs = []
    for kind in range(3):
        part = P_DN_QKV + kind
        in_specs += [
            pl.BlockSpec((1, DN_HALO, GROUP_WIDTH),
                         lambda b, t, part=part: (b, jnp.maximum(t * nh - 1, 0), part)),
            pl.BlockSpec((1, tl, GROUP_WIDTH), lambda b, t, part=part: (b, t, part)),
            pl.BlockSpec((1, DN_HALO, GROUP_WIDTH),
                         lambda b, t, part=part: (b, jnp.minimum((t + 1) * nh, last), part)),
        ]
    in_specs.append(pl.BlockSpec((DN_CONV, 3 * GROUP_WIDTH), lambda b, t: (0, 0)))
    ospec = pl.BlockSpec((1, tl, GROUP_WIDTH), lambda b, t: (b, t, 0))
    return pl.pallas_call(
        functools.partial(_dn_prep_kernel, tl=tl),
        grid=(B, L // tl),
        in_specs=in_specs,
        out_specs=[ospec, ospec, ospec],
        out_shape=[jax.ShapeDtypeStruct((B, L, GROUP_WIDTH), F32)] * 3,
        compiler_params=_cparams("parallel", "parallel"),
        name="dn_prep",
    )(*([u] * 9), conv_w)


def _softplus(x):
    return jnp.maximum(x, 0.0) + jnp.log1p(jnp.exp(-jnp.abs(x)))


def _dn_gates_kernel(ab_ref, alog_ref, dtb_ref, cols_ref, rows_ref, *, T):
    lane = _iota2((1, HEAD_DIM), 1)
    neg_a = -jnp.exp(alog_ref[...])
    tri_f, tri_b, eye4 = _tri4(True), _tri4(False), _eye4()

    def body(it, carry):
        st = []
        for j in range(DN_GATE_GROUP):
            c = it * DN_GATE_GROUP + j
            rows = pl.ds(pl.multiple_of(c * CHUNK, CHUNK), CHUNK)
            ab = ab_ref[0, rows, :]
            st.append(dict(c=c, rows=rows, ab=ab, g3=_stack3(neg_a * _softplus(ab + dtb_ref[...]))))
        for s in st:
            s["G"] = jnp.where(lane < N_HEADS, _dot(tri_f, s["g3"]), _dot(tri_b, s["g3"]))
        for s in st:
            cols_ref[0, s["rows"], :] = jnp.where(lane < 2 * N_HEADS, s["G"], _sigmoid(s["ab"]))
            rows_ref[0, s["c"]] = _dot_tn(_stack3(s["G"]), eye4)[0:4 * N_HEADS]
        return carry
    lax.fori_loop(0, T // CHUNK // DN_GATE_GROUP, body, 0)


def _dn_gates(uab, a_log, dt_bias, T=1024):
    B, L, _ = uab.shape
    pad = HEAD_DIM - 2 * N_HEADS
    alog = jnp.pad(a_log.astype(F32).reshape(1, 2 * N_HEADS), ((0, 0), (0, pad)))
    dtb = jnp.pad(dt_bias.astype(F32).reshape(1, 2 * N_HEADS), ((0, 0), (0, pad)))
    pspec = pl.BlockSpec((1, HEAD_DIM), lambda b, t: (0, 0))
    return pl.pallas_call(
        functools.partial(_dn_gates_kernel, T=T),
        grid=(B, L // T),
        in_specs=[pl.BlockSpec((1, T, HEAD_DIM), lambda b, t: (b, t, 0)), pspec, pspec],
        out_specs=[pl.BlockSpec((1, T, HEAD_DIM), lambda b, t: (b, t, 0)),
                   pl.BlockSpec((1, T // CHUNK, 4 * N_HEADS, HEAD_DIM), lambda b, t: (b, t, 0, 0))],
        out_shape=[jax.ShapeDtypeStruct((B, L, HEAD_DIM), F32),
                   jax.ShapeDtypeStruct((B, L // CHUNK, 4 * N_HEADS, HEAD_DIM), F32)],
        compiler_params=_cparams("parallel", "parallel"),
        name="dn_gates",
    )(uab, alog, dtb)


def _lhs3(x):
    hi, lo = _split2(x)
    return jnp.concatenate([hi, lo], axis=1)


def _rhs3(x):
    hi, lo = _split2(x)
    return jnp.concatenate([hi, lo, hi, jnp.zeros_like(hi)], axis=0)


def _dn_kernel(qf_ref, qb_ref, kf_ref, kb_ref, vf_ref, vb_ref, gcf_ref, gcb_ref, grf_ref, grb_ref,
               of_ref, ob_ref, s_ref, *, T, HP):
    hg = pl.program_id(1)

    @pl.when(pl.program_id(2) == 0)
    def _():
        s_ref[...] = jnp.zeros_like(s_ref)

    n_chunks = T // CHUNK
    lane = _iota2((1, HEAD_DIM), 1)
    eye = jnp.where(_iota2((CHUNK, HEAD_DIM), 0) == (_iota2((CHUNK, HEAD_DIM), 1) & (CHUNK - 1)), 1.0, 0.0)

    chains = []
    for hh in range(HP):
        for d in range(2):
            fwd = d == 0
            incl, strict = _order_masks(fwd, HEAD_DIM)
            chains.append(dict(
                idx=2 * hh + d, cols=slice(hh * HEAD_DIM, (hh + 1) * HEAD_DIM), fwd=fwd,
                col=d * N_HEADS + hg * HP + hh, incl=incl, strict=strict,
                q=(qf_ref, qb_ref)[d], k=(kf_ref, kb_ref)[d], v=(vf_ref, vb_ref)[d],
                gc=(gcf_ref, gcb_ref)[d], gr=(grf_ref, grb_ref)[d], o=(of_ref, ob_ref)[d]))

    def body(step, carry):
        st = []
        for ch in chains:
            cols, col = ch["cols"], ch["col"]
            c = step if ch["fwd"] else n_chunks - 1 - step
            rows = pl.ds(pl.multiple_of(c * CHUNK, CHUNK), CHUNK)
            gates = ch["gc"][0, rows, :]
            k = ch["k"][0, rows, cols]
            kh = k.astype(BF16)
            st.append(dict(
                rows=rows, q=ch["q"][0, rows, cols], k=k, v=ch["v"][0, rows, cols],
                k2=jnp.concatenate([kh, kh], axis=0),
                g_col=jnp.sum(jnp.where(lane == col, gates, 0.0), axis=-1, keepdims=True),
                beta=jnp.sum(jnp.where(lane == col + 2 * N_HEADS, gates, 0.0), axis=-1, keepdims=True),
                g_row=ch["gr"][0, c, pl.ds(col, 1), :]))
        for ch, s in zip(chains, st):
            s["gamma"] = jnp.where(ch["incl"], jnp.exp(jnp.minimum(s["g_col"] - s["g_row"], 0.0)), 0.0)
            s["kb"] = s["k"] * s["beta"]
        for ch, s in zip(chains, st):
            n_mat = jnp.where(ch["strict"], _dot_nt(s["kb"].astype(BF16), s["k2"]) * s["gamma"], 0.0)
            s["x"] = eye - n_mat
            s["nl"], s["nr"] = _lhs3(n_mat), _rhs3(n_mat)

        order = 1
        while True:
            for s in st:
                s["nk"] = _dot(s["nl"], s["nr"])
            for s in st:
                s["nr"] = _rhs3(s["nk"])
                s["x"] = s["x"] + _dot(_lhs3(s["x"]), s["nr"])
            order *= 2
            if 2 * order >= CHUNK:
                break
            for s in st:
                s["nl"] = _lhs3(s["nk"])

        for s in st:
            s["eg"] = jnp.exp(s["g_col"])
            rhs = jnp.concatenate([s["kb"] * s["eg"], s["v"] * s["beta"]], axis=1)
            s["wu"] = _dot(_lhs3(s["x"]), _rhs3(rhs))
        for ch, s in zip(chains, st):
            s["a_qk"] = (_dot_nt(s["q"].astype(BF16), s["k2"]) * s["gamma"])[:, 0:CHUNK].astype(BF16)
            s["s"] = s_ref[ch["idx"]]
            s["sb"] = s["s"].astype(BF16)
        for s in st:
            s["v_new"] = (s["wu"][:, HEAD_DIM:] - _dot(s["wu"][:, :HEAD_DIM].astype(BF16), s["sb"])).astype(BF16)
        for ch, s in zip(chains, st):
            ch["o"][0, s["rows"], ch["cols"]] = (_dot((s["q"] * s["eg"]).astype(BF16), s["sb"])
                                                 + _dot(s["a_qk"], s["v_new"]))
        for ch, s in zip(chains, st):
            g_last = s["g_col"][CHUNK - 1:CHUNK] if ch["fwd"] else s["g_col"][0:1]
            k_dec = (s["k"] * jnp.exp(g_last - s["g_col"])).astype(BF16)
            s_ref[ch["idx"]] = jnp.exp(g_last) * s["s"] + _dot_tn(k_dec, s["v_new"])
        return carry
    lax.fori_loop(0, n_chunks, body, 0)


def _dn_scan(qh, kh, vh, gate_cols, gate_rows, T=512, HP=4):
    B, L, _ = qh.shape
    n = L // T
    hf, hb = _bidir_specs(T, n, HEAD_DIM, 0, 0, HP)
    gcf = pl.BlockSpec((1, T, HEAD_DIM), lambda b, hg, t: (b, t, 0))
    gcb = pl.BlockSpec((1, T, HEAD_DIM), lambda b, hg, t: (b, n - 1 - t, 0))
    rshape = (1, T // CHUNK, 4 * N_HEADS, HEAD_DIM)
    grf = pl.BlockSpec(rshape, lambda b, hg, t: (b, t, 0, 0))
    grb = pl.BlockSpec(rshape, lambda b, hg, t: (b, n - 1 - t, 0, 0))
    oshape = jax.ShapeDtypeStruct((B, L, GROUP_WIDTH), F32)
    return pl.pallas_call(
        functools.partial(_dn_kernel, T=T, HP=HP),
        grid=(B, N_HEADS // HP, n),
        in_specs=[hf, hb, hf, hb, hf, hb, gcf, gcb, grf, grb],
        out_specs=[hf, hb],
        out_shape=[oshape, oshape],
        scratch_shapes=[pltpu.VMEM((2 * HP, HEAD_DIM, HEAD_DIM), F32)],
        compiler_params=_cparams("parallel", "parallel", "arbitrary"),
        name="dn_scan",
    )(qh, qh, kh, kh, vh, vh, gate_cols, gate_cols, gate_rows, gate_rows)


def _encoder_layer(x, mod, layer, p, rope_tabs):
    u, uab = _in_proj(x, p["norm1_g"][layer], mod, p["w_in_main"], p["w_in_ab"][layer], layer)

    swa_q, swa_k = _qk_prep(u, P_SWA_Q, P_SWA_K, None, p["swa_q_norm"][layer], p["swa_k_norm"][layer],
                            rope_tabs, F32)
    y_swa = _swa_attention(swa_q, swa_k, u)

    o_hg = _hgrn_scan(u, p["hgrn_lb_logits"], layer)

    na_q, na_k, na_v = _qk_prep(u, P_NA_Q, P_NA_K, P_NA_V, p["na_q_norm"][layer], p["na_k_norm"][layer],
                                None, BF16)
    y_na = _na_attention(na_q, na_k, na_v, p["na_bias"][layer])

    dq, dk, dv = _dn_prep(u, p["dn_conv_w"][layer])
    gate_cols, gate_rows = _dn_gates(uab, p["dn_a_log"][layer], p["dn_dt_bias"][layer])
    o_dn = _dn_scan(dq, dk, dv, gate_cols, gate_rows)

    x = _out_proj(x, mod, u, y_swa, y_na, o_hg, p["hgrn_norm_g"][layer], o_dn, p["dn_norm_g"][layer],
                  p["w_out"], layer)
    return _mlp(x, p["norm2_g"][layer], mod, p["w_mlp_in"], p["w_mlp_out"], layer)


def _run_trunk(x, mod_all, p):
    B, L, _ = x.shape
    rope_tabs = _rope_tables(L)
    for layer in range(DEPTH):
        x = _encoder_layer(x, mod_all[layer].reshape(B, 6, D_MODEL), layer, p, rope_tabs)
    return x


def kernel(x_prompt, x_sample, c_prompt, c_sample, norm1_g, norm2_g, ada_w, ada_b, w_in, w_out,
           swa_q_norm, swa_k_norm, hgrn_lb_logits, hgrn_norm_g, na_q_norm, na_k_norm, na_rpb,
           dn_conv_w, dn_a_log, dn_dt_bias, dn_norm_g, w_mlp_in, w_mlp_out):
    bp, bs = c_prompt.shape[0], c_sample.shape[0]
    pad_rows = -(bp + bs) % 8
    c_all = jnp.concatenate([c_prompt, c_sample, jnp.zeros((pad_rows, D_MODEL), F32)], axis=0)
    mod_all = _modulation(c_all, ada_w, ada_b)

    p = dict(
        norm1_g=norm1_g, norm2_g=norm2_g,
        w_in_main=w_in.astype(BF16),
        w_in_ab=jnp.pad(w_in[:, :, IN_MAIN:], ((0, 0), (0, 0), (0, HEAD_DIM - 4 * N_HEADS))).astype(BF16),
        w_out=w_out.astype(BF16),
        swa_q_norm=swa_q_norm, swa_k_norm=swa_k_norm,
        hgrn_lb_logits=hgrn_lb_logits.astype(F32), hgrn_norm_g=hgrn_norm_g,
        na_q_norm=na_q_norm, na_k_norm=na_k_norm,
        na_bias=jnp.stack([_na_bias_table(na_rpb[l]) for l in range(DEPTH)]),
        dn_conv_w=dn_conv_w, dn_a_log=dn_a_log, dn_dt_bias=dn_dt_bias, dn_norm_g=dn_norm_g,
        w_mlp_in=w_mlp_in.astype(BF16), w_mlp_out=w_mlp_out.astype(BF16),
    )
    y_prompt = _run_trunk(x_prompt, mod_all[:, :bp], p)
    y_sample = _run_trunk(x_sample, mod_all[:, bp:bp + bs], p)
    return (y_prompt, y_sample)
```

```python
import functools
import math

import jax
import jax.numpy as jnp
from jax import lax
from jax.experimental import pallas as pl
from jax.experimental.pallas import tpu as pltpu

F32 = jnp.float32
BF16 = jnp.bfloat16

D_MODEL = 2048
DEPTH = 4
HEAD_DIM = 128
N_HEADS = 4
GROUP_WIDTH = N_HEADS * HEAD_DIM
IN_MAIN = 15 * GROUP_WIDTH
D_FF = 4 * D_MODEL
NORM_EPS = 1e-6
ROPE_THETA = 500000.0
ROPE_DIM = HEAD_DIM // 4
DILATED_CONFIGS = ((128, 1), (512, 4), (2048, 16))
GRID_W = 64
NA_KH = 8
NA_KW = 16
CHUNK = 64
SUB = 16
DN_CONV = 5
MASK_VALUE = -1e30
LOG_FLOOR = 1e-30
ATTN_SCALE = HEAD_DIM ** -0.5
EXP_CLAMP = 60.0
HIGHEST = lax.Precision.HIGHEST

P_SWA_Q, P_SWA_K, P_SWA_V = 0, 1, 2
P_HG_Q, P_HG_FF, P_HG_FB, P_HG_I, P_HG_G = 3, 4, 5, 6, 7
P_NA_Q, P_NA_K, P_NA_V = 8, 9, 10
P_DN_QKV, P_DN_Z = 11, 14

VMEM_LIMIT = 56 * 1024 * 1024


def _cparams(*sem):
    return pltpu.CompilerParams(dimension_semantics=sem, vmem_limit_bytes=VMEM_LIMIT)


def _dot(a, b):
    return jnp.dot(a, b, preferred_element_type=F32)


def _dot_nt(a, b):
    return lax.dot_general(a, b, (((1,), (1,)), ((), ())), preferred_element_type=F32)


def _dot_tn(a, b):
    return lax.dot_general(a, b, (((0,), (0,)), ((), ())), preferred_element_type=F32)


def _sigmoid(x):
    return jax.nn.sigmoid(x)


def _silu(x):
    return x * _sigmoid(x)


def _rms(x):
    return x * lax.rsqrt(jnp.mean(x * x, axis=-1, keepdims=True) + NORM_EPS)


def _split2(x):
    hi = x.astype(BF16)
    return hi, (x - hi.astype(F32)).astype(BF16)


def _split3(x):
    hi = x.astype(BF16)
    r1 = x - hi.astype(F32)
    mid = r1.astype(BF16)
    return hi, mid, (r1 - mid.astype(F32)).astype(BF16)


def _stack3(x):
    hi, mid, lo = _split3(x)
    return jnp.concatenate([hi, mid, lo, jnp.zeros_like(hi)], axis=0)


def _iota2(shape, axis):
    return lax.broadcasted_iota(jnp.int32, shape, axis)


def _tri4(fwd):
    ri, ci = _iota2((CHUNK, 4 * CHUNK), 0), _iota2((CHUNK, 4 * CHUNK), 1)
    cm = ci & (CHUNK - 1)
    order = (cm <= ri) if fwd else (cm >= ri)
    return jnp.where(order & (ci < 3 * CHUNK), 1.0, 0.0).astype(BF16)


def _eye4():
    ri, ci = _iota2((4 * CHUNK, 2 * CHUNK), 0), _iota2((4 * CHUNK, 2 * CHUNK), 1)
    same = (ri & (CHUNK - 1)) == (ci & (CHUNK - 1))
    return jnp.where(same & (ri < 3 * CHUNK), 1.0, 0.0).astype(BF16)


def _order_masks(fwd, width):
    ri, ci = _iota2((CHUNK, width), 0), _iota2((CHUNK, width), 1) & (CHUNK - 1)
    if fwd:
        return ci <= ri, ci < ri
    return ci >= ri, ci > ri


def _mod_kernel(c_ref, w_ref, b_ref, o_ref):
    a = _silu(c_ref[...]).astype(BF16)
    o_ref[0] = _dot(a, w_ref[0].astype(BF16)) + b_ref[0]


def _modulation(c_all, ada_w, ada_b):
    rows, d = c_all.shape
    n = ada_w.shape[-1]
    tn = 1024
    return pl.pallas_call(
        _mod_kernel,
        grid=(DEPTH, n // tn),
        in_specs=[
            pl.BlockSpec((rows, d), lambda l, j: (0, 0)),
            pl.BlockSpec((1, d, tn), lambda l, j: (l, 0, j)),
            pl.BlockSpec((1, 1, tn), lambda l, j: (l, 0, j)),
        ],
        out_specs=pl.BlockSpec((1, rows, tn), lambda l, j: (l, 0, j)),
        out_shape=jax.ShapeDtypeStruct((DEPTH, rows, n), F32),
        compiler_params=_cparams("parallel", "parallel"),
        name="adaln_mod",
    )(c_all, ada_w, ada_b.reshape(DEPTH, 1, n))


NORM_ROWS = 128


def _norm_mod_store(h_ref, x_ref, g, scale, shift, tm):
    gain = g * (1.0 + scale)

    def body(c, carry):
        rows = pl.ds(pl.multiple_of(c * NORM_ROWS, NORM_ROWS), NORM_ROWS)
        x = x_ref[0, rows, :]
        inv = lax.rsqrt(jnp.mean(x * x, axis=-1, keepdims=True) + NORM_EPS)
        h_ref[rows, :] = (x_ref[0, rows, :] * inv * gain + shift).astype(BF16)
        return carry
    lax.fori_loop(0, tm // NORM_ROWS, body, 0)


def _in_proj_kernel(x_ref, g_ref, mod_ref, w_ref, wab_ref, u_ref, uab_ref, h_ref, *, tm):
    @pl.when(pl.program_id(2) == 0)
    def _():
        _norm_mod_store(h_ref, x_ref, g_ref[...], mod_ref[0, 1:2, :], mod_ref[0, 0:1, :], tm)
        uab_ref[0] = _dot(h_ref[...], wab_ref[...])
    u_ref[0] = _dot(h_ref[...], w_ref[...])


def _in_proj(x, g, mod, w_main, w_ab, layer, tm=512, tn=1536):
    B, L, D = x.shape
    N = IN_MAIN
    return pl.pallas_call(
        functools.partial(_in_proj_kernel, tm=tm),
        grid=(B, L // tm, N // tn),
        in_specs=[
            pl.BlockSpec((1, tm, D), lambda b, i, j: (b, i, 0)),
            pl.BlockSpec((1, D), lambda b, i, j: (0, 0)),
            pl.BlockSpec((1, 6, D), lambda b, i, j: (b, 0, 0)),
            pl.BlockSpec((None, D, tn), lambda b, i, j: (layer, 0, j)),
            pl.BlockSpec((D, HEAD_DIM), lambda b, i, j: (0, 0)),
        ],
        out_specs=[
            pl.BlockSpec((1, tm, tn), lambda b, i, j: (b, i, j)),
            pl.BlockSpec((1, tm, HEAD_DIM), lambda b, i, j: (b, i, 0)),
        ],
        out_shape=[jax.ShapeDtypeStruct((B, L, N), F32),
                   jax.ShapeDtypeStruct((B, L, HEAD_DIM), F32)],
        scratch_shapes=[pltpu.VMEM((tm, D), BF16)],
        compiler_params=_cparams("parallel", "parallel", "arbitrary"),
        name="in_proj",
    )(x, g.reshape(1, D), mod, w_main, w_ab)


def _gated_heads(of_ref, ob_ref, z_ref, g_ref):
    parts = []
    for h in range(N_HEADS):
        cols = slice(h * HEAD_DIM, (h + 1) * HEAD_DIM)
        o = of_ref[0, :, cols] + ob_ref[0, :, cols]
        parts.append((_rms(o) * g_ref[...] * _silu(z_ref[0, :, cols])).astype(BF16))
    return jnp.concatenate(parts, axis=1)


def _out_proj_kernel(x_ref, mod_ref, swa_ref, na_ref, hgf_ref, hgb_ref, hgz_ref, hgg_ref,
                     dnf_ref, dnb_ref, dnz_ref, dng_ref, w_ref, o_ref):
    ys = (swa_ref[0], _gated_heads(hgf_ref, hgb_ref, hgz_ref, hgg_ref),
          na_ref[0], _gated_heads(dnf_ref, dnb_ref, dnz_ref, dng_ref))
    acc = None
    for m, y in enumerate(ys):
        part = _dot(y, w_ref[m * GROUP_WIDTH:(m + 1) * GROUP_WIDTH, :])
        acc = part if acc is None else acc + part
    o_ref[0] = x_ref[0] + mod_ref[0, 2:3, :] * acc


def _out_proj(x, mod, u, y_swa, y_na, o_hg, g_hg, o_dn, g_dn, w_out, layer, tm=512):
    B, L, D = x.shape
    yspec = pl.BlockSpec((1, tm, GROUP_WIDTH), lambda b, i: (b, i, 0))
    gspec = pl.BlockSpec((1, HEAD_DIM), lambda b, i: (0, 0))
    zspec = lambda part: pl.BlockSpec((1, tm, GROUP_WIDTH), lambda b, i: (b, i, part))
    return pl.pallas_call(
        _out_proj_kernel,
        grid=(B, L // tm),
        in_specs=[
            pl.BlockSpec((1, tm, D), lambda b, i: (b, i, 0)),
            pl.BlockSpec((1, 6, D), lambda b, i: (b, 0, 0)),
            yspec, yspec,
            yspec, yspec, zspec(P_HG_G), gspec,
            yspec, yspec, zspec(P_DN_Z), gspec,
            pl.BlockSpec((None, D, D), lambda b, i: (layer, 0, 0)),
        ],
        out_specs=pl.BlockSpec((1, tm, D), lambda b, i: (b, i, 0)),
        out_shape=jax.ShapeDtypeStruct((B, L, D), F32),
        compiler_params=_cparams("parallel", "parallel"),
        name="out_proj",
    )(x, mod, y_swa, y_na, *o_hg, u, g_hg.reshape(1, HEAD_DIM), *o_dn, u, g_dn.reshape(1, HEAD_DIM), w_out)


def _mlp_kernel(x_ref, g_ref, mod_ref, w1_ref, w2_ref, o_ref, h_ref, *, tm):
    j = pl.program_id(2)

    @pl.when(j == 0)
    def _():
        _norm_mod_store(h_ref, x_ref, g_ref[...], mod_ref[0, 4:5, :], mod_ref[0, 3:4, :], tm)
        o_ref[...] = jnp.zeros_like(o_ref)

    half = w1_ref.shape[1] // 2
    hids = [jnp.square(jnp.maximum(_dot(h_ref[...], w1_ref[:, c * half:(c + 1) * half]), 0.0)).astype(BF16)
            for c in range(2)]
    o_ref[0] += _dot(hids[0], w2_ref[0:half, :]) + _dot(hids[1], w2_ref[half:, :])

    @pl.when(j == pl.num_programs(2) - 1)
    def _():
        o_ref[0] = x_ref[0] + mod_ref[0, 5:6, :] * o_ref[0]


def _mlp(x, g, mod, w1, w2, layer, tm=512, tf=1024):
    B, L, D = x.shape
    F = w1.shape[-1]
    return pl.pallas_call(
        functools.partial(_mlp_kernel, tm=tm),
        grid=(B, L // tm, F // tf),
        in_specs=[
            pl.BlockSpec((1, tm, D), lambda b, i, j: (b, i, 0)),
            pl.BlockSpec((1, D), lambda b, i, j: (0, 0)),
            pl.BlockSpec((1, 6, D), lambda b, i, j: (b, 0, 0)),
            pl.BlockSpec((None, D, tf), lambda b, i, j: (layer, 0, j)),
            pl.BlockSpec((None, tf, D), lambda b, i, j: (layer, j, 0)),
        ],
        out_specs=pl.BlockSpec((1, tm, D), lambda b, i, j: (b, i, 0)),
        out_shape=jax.ShapeDtypeStruct((B, L, D), F32),
        scratch_shapes=[pltpu.VMEM((tm, D), BF16)],
        compiler_params=_cparams("parallel", "parallel", "arbitrary"),
        name="mlp",
    )(x, g.reshape(1, D), mod, w1, w2)


def _qk_prep_kernel(*refs, rope):
    if rope:
        q_ref, k_ref, gq_ref, gk_ref, c_ref, s1_ref, s2_ref, qo_ref, ko_ref = refs
    else:
        q_ref, k_ref, v_ref, gq_ref, gk_ref, qo_ref, ko_ref, vo_ref = refs
        vo_ref[...] = v_ref[...].astype(vo_ref.dtype)
    for src, g_ref, dst in ((q_ref, gq_ref, qo_ref), (k_ref, gk_ref, ko_ref)):
        for h in range(N_HEADS):
            cols = slice(h * HEAD_DIM, (h + 1) * HEAD_DIM)
            y = _rms(src[0, :, cols]) * g_ref[...]
            if rope:
                y = (y * c_ref[...] + pltpu.roll(y, ROPE_DIM // 2, 1) * s1_ref[...]
                     + pltpu.roll(y, HEAD_DIM - ROPE_DIM // 2, 1) * s2_ref[...])
            dst[0, :, cols] = y.astype(dst.dtype)


def _rope_tables(L):
    half = ROPE_DIM // 2
    inv_freq = jnp.power(ROPE_THETA, -jnp.arange(half, dtype=F32) / half)
    ang = jnp.arange(L).astype(F32)[:, None] * inv_freq[None, :]
    cos, sin = jnp.cos(ang), jnp.sin(ang)
    zeros = jnp.zeros((L, HEAD_DIM - ROPE_DIM), F32)
    zh = jnp.zeros((L, half), F32)
    c = jnp.concatenate([cos, cos, jnp.ones((L, HEAD_DIM - ROPE_DIM), F32)], axis=1)
    s1 = jnp.concatenate([zh, sin, zeros], axis=1)
    s2 = jnp.concatenate([-sin, zh, zeros], axis=1)
    return c, s1, s2


def _qk_prep(u, part_q, part_k, part_v, gq, gk, rope_tabs, out_dtype, tl=512):
    B, L, _ = u.shape
    rope = rope_tabs is not None
    uspec = lambda part: pl.BlockSpec((1, tl, GROUP_WIDTH), lambda b, t, part=part: (b, t, part))
    gspec = pl.BlockSpec((1, HEAD_DIM), lambda b, t: (0, 0))
    tspec = pl.BlockSpec((tl, HEAD_DIM), lambda b, t: (t, 0))
    ospec = pl.BlockSpec((1, tl, GROUP_WIDTH), lambda b, t: (b, t, 0))
    gains = [gq.reshape(1, HEAD_DIM), gk.reshape(1, HEAD_DIM)]
    if rope:
        in_specs = [uspec(part_q), uspec(part_k), gspec, gspec, tspec, tspec, tspec]
        args = [u, u] + gains + list(rope_tabs)
    else:
        in_specs = [uspec(part_q), uspec(part_k), uspec(part_v), gspec, gspec]
        args = [u, u, u] + gains
    n_out = 2 if rope else 3
    return pl.pallas_call(
        functools.partial(_qk_prep_kernel, rope=rope),
        grid=(B, L // tl),
        in_specs=in_specs,
        out_specs=[ospec] * n_out,
        out_shape=[jax.ShapeDtypeStruct((B, L, GROUP_WIDTH), out_dtype)] * n_out,
        compiler_params=_cparams("parallel", "parallel"),
        name="qk_prep_rope" if rope else "qkv_prep",
    )(*args)


SWA_Q = 128
SWA_K = 2 * SWA_Q
SWA_RADIUS = 64
SWA_ROWS = 256
SWA_GROUP = 4


def _swa_kernel(q_ref, k_ref, v_ref, o_ref, ob_ref, ls_ref, mask_ref, *, L, T):
    t0 = pl.program_id(2) * T
    diff = _iota2((SWA_Q, SWA_K), 1) - _iota2((SWA_Q, SWA_K), 0)
    for i in range(3):
        mask_ref[i] = jnp.where(jnp.abs(diff - i * SWA_RADIUS) <= SWA_RADIUS, 0.0, MASK_VALUE)

    for bi, (window, dil) in enumerate(DILATED_CONFIGS):
        assert window // (2 * dil) == SWA_RADIUS
        span = SWA_Q * dil
        n_sub = T // span

        def body(it, carry, dil=dil, bi=bi, span=span, n_sub=n_sub):
            st = []
            for j in range(SWA_GROUP):
                idx = it * SWA_GROUP + j
                r, loc = idx // n_sub, (idx % n_sub) * span
                base = jnp.clip(t0 + loc - SWA_RADIUS * dil, 0, L - SWA_K * dil)
                if dil == 1:
                    qs = pl.ds(pl.multiple_of(loc, SWA_Q), SWA_Q)
                    ks = pl.ds(pl.multiple_of(base, SWA_RADIUS), SWA_K)
                else:
                    qs = pl.ds(loc + r, SWA_Q, stride=dil)
                    ks = pl.ds(base + r, SWA_K, stride=dil)
                st.append(dict(qs=qs, ks=ks, mask=(t0 + loc - base) // (dil * SWA_RADIUS)))
            for s in st:
                sc = _dot_nt(q_ref[0, s["qs"], :].astype(BF16), k_ref[0, s["ks"], :].astype(BF16))
                s["s"] = sc * ATTN_SCALE + mask_ref[s["mask"]]
            for s in st:
                s["m"] = jnp.max(s["s"], axis=-1, keepdims=True)
                s["e"] = jnp.exp(s["s"] - s["m"])
                s["den"] = jnp.sum(s["e"], axis=-1, keepdims=True)
            for s in st:
                pv = _dot(s["e"].astype(BF16), v_ref[0, s["ks"], :].astype(BF16))
                ob_ref[bi, s["qs"], :] = pv / s["den"]
                ls_ref[bi, s["qs"], :] = jnp.broadcast_to(s["m"] + jnp.log(s["den"]), (SWA_Q, HEAD_DIM))
            return carry
        lax.fori_loop(0, T // SWA_Q // SWA_GROUP, body, 0)

    def combine(c, carry):
        rows = pl.ds(pl.multiple_of(c * SWA_ROWS, SWA_ROWS), SWA_ROWS)
        l0, l1, l2 = ls_ref[0, rows, :], ls_ref[1, rows, :], ls_ref[2, rows, :]
        mx = jnp.maximum(l0, jnp.maximum(l1, l2))
        w0, w1, w2 = jnp.exp(l0 - mx), jnp.exp(l1 - mx), jnp.exp(l2 - mx)
        num = w0 * ob_ref[0, rows, :] + w1 * ob_ref[1, rows, :] + w2 * ob_ref[2, rows, :]
        o_ref[0, rows, :] = (num / (w0 + w1 + w2)).astype(o_ref.dtype)
        return carry
    lax.fori_loop(0, T // SWA_ROWS, combine, 0)


def _swa_attention(qn, kn, u, T=2048):
    B, L, _ = qn.shape
    max_dil = max(d for _, d in DILATED_CONFIGS)
    assert L % T == 0 and T % (SWA_Q * max_dil) == 0 and L >= SWA_K * max_dil
    nb = len(DILATED_CONFIGS)
    return pl.pallas_call(
        functools.partial(_swa_kernel, L=L, T=T),
        grid=(B, N_HEADS, L // T),
        in_specs=[
            pl.BlockSpec((1, T, HEAD_DIM), lambda b, h, t: (b, t, h)),
            pl.BlockSpec((1, L, HEAD_DIM), lambda b, h, t: (b, 0, h)),
            pl.BlockSpec((1, L, HEAD_DIM), lambda b, h, t: (b, 0, P_SWA_V * N_HEADS + h)),
        ],
        out_specs=pl.BlockSpec((1, T, HEAD_DIM), lambda b, h, t: (b, t, h)),
        out_shape=jax.ShapeDtypeStruct((B, L, GROUP_WIDTH), BF16),
        scratch_shapes=[pltpu.VMEM((nb, T, HEAD_DIM), F32), pltpu.VMEM((nb, T, HEAD_DIM), F32),
                        pltpu.VMEM((3, SWA_Q, SWA_K), F32)],
        compiler_params=_cparams("parallel", "parallel", "arbitrary"),
        name="swa_attention",
    )(qn, kn, u)


NA_KEYS = NA_KH * GRID_W
NA_GROUP = 8


def _na_bias_table(rpb):
    cq = jnp.arange(GRID_W)
    ck = jnp.arange(GRID_W)
    c0 = jnp.clip(cq - NA_KW // 2, 0, GRID_W - NA_KW)
    col_ok = (ck[None, :] >= c0[:, None]) & (ck[None, :] < c0[:, None] + NA_KW)
    col_off = jnp.clip(ck[None, :] - cq[:, None], -(NA_KW - 1), NA_KW - 1) + (NA_KW - 1)
    row_off = jnp.arange(NA_KH)[:, None] + jnp.arange(NA_KH)[None, :]
    bias = rpb.astype(F32)[:, row_off]
    bias = bias[..., col_off]
    bias = bias.transpose(0, 1, 3, 2, 4)
    bias = jnp.where(col_ok[None, None, :, None, :], bias, MASK_VALUE)
    return bias.reshape(rpb.shape[0], NA_KH, GRID_W, NA_KEYS)


def _na_kernel(q_ref, k_ref, v_ref, bias_ref, o_ref, *, rows, R):
    t = pl.program_id(2)

    def body(it, carry):
        st = []
        for j in range(NA_GROUP):
            i = it * NA_GROUP + j
            r = t * R + i
            r0 = jnp.clip(r - NA_KH // 2, 0, rows - NA_KH)
            st.append(dict(shift=r0 - r + (NA_KH - 1),
                           qs=pl.ds(pl.multiple_of(i * GRID_W, GRID_W), GRID_W),
                           ks=pl.ds(pl.multiple_of(r0 * GRID_W, GRID_W), NA_KEYS)))
        for s in st:
            s["s"] = _dot_nt(q_ref[0, s["qs"], :], k_ref[0, s["ks"], :]) * ATTN_SCALE + bias_ref[0, s["shift"]]
        for s in st:
            s["m"] = jnp.max(s["s"], axis=-1, keepdims=True)
            s["e"] = jnp.exp(s["s"] - s["m"])
            s["den"] = jnp.sum(s["e"], axis=-1, keepdims=True)
        for s in st:
            o_ref[0, s["qs"], :] = (_dot(s["e"].astype(BF16), v_ref[0, s["ks"], :]) / s["den"]).astype(o_ref.dtype)
        return carry
    lax.fori_loop(0, R // NA_GROUP, body, 0)


def _na_attention(qn, kn, vn, bias, R=16):
    B, L, _ = qn.shape
    rows = L // GRID_W
    assert rows >= NA_KH and rows % R == 0
    T = R * GRID_W
    return pl.pallas_call(
        functools.partial(_na_kernel, rows=rows, R=R),
        grid=(B, N_HEADS, rows // R),
        in_specs=[
            pl.BlockSpec((1, T, HEAD_DIM), lambda b, h, t: (b, t, h)),
            pl.BlockSpec((1, L, HEAD_DIM), lambda b, h, t: (b, 0, h)),
            pl.BlockSpec((1, L, HEAD_DIM), lambda b, h, t: (b, 0, h)),
            pl.BlockSpec((1, NA_KH, GRID_W, NA_KEYS), lambda b, h, t: (h, 0, 0, 0)),
        ],
        out_specs=pl.BlockSpec((1, T, HEAD_DIM), lambda b, h, t: (b, t, h)),
        out_shape=jax.ShapeDtypeStruct((B, L, GROUP_WIDTH), BF16),
        compiler_params=_cparams("parallel", "parallel", "arbitrary"),
        name="na_attention",
    )(qn, kn, vn, bias)


def _bidir_specs(T, n, width, part_f, part_b, heads_per_step, batch_per_step=1):
    w = heads_per_step * width
    bb = batch_per_step
    return (pl.BlockSpec((bb, T, w), lambda b, hg, t: (b, t, part_f * (GROUP_WIDTH // w) + hg)),
            pl.BlockSpec((bb, T, w), lambda b, hg, t: (b, n - 1 - t, part_b * (GROUP_WIDTH // w) + hg)))


def _hgrn_kernel(qf_ref, qb_ref, ff_ref, fb_ref, vf_ref, vb_ref, lg_ref, of_ref, ob_ref,
                 s_ref, s0_ref, kk_ref, b_ref, *, layer, T, HP, BB):
    @pl.when(pl.program_id(2) == 0)
    def _():
        s_ref[...] = jnp.zeros_like(s_ref)

    n_chunks = T // CHUNK
    n_sub = CHUNK // SUB
    ci = _iota2((CHUNK, CHUNK), 1)

    chains = []
    for hh in range(HP):
        cols = slice(hh * HEAD_DIM, (hh + 1) * HEAD_DIM)
        for d in range(2):
            lg = lg_ref[:, d, 0, cols]
            ex = jnp.exp(lg - jnp.max(lg, axis=0, keepdims=True))
            p = ex / jnp.sum(ex, axis=0, keepdims=True)
            cs = p[0:1]
            for i in range(1, layer + 1):
                cs = cs + p[i:i + 1]
            fwd = d == 0
            for bi in range(BB):
                chains.append(dict(
                    idx=(bi * HP + hh) * 2 + d, bi=bi, cols=cols, fwd=fwd, lb=cs - p[0:1],
                    q=(qf_ref, qb_ref)[d], f=(ff_ref, fb_ref)[d], v=(vf_ref, vb_ref)[d], o=(of_ref, ob_ref)[d],
                    tri=_tri4(fwd), incl=_order_masks(fwd, CHUNK)[0]))

    def exact_scores(qq, kk, b):
        kk_ref[...] = kk
        b_ref[...] = b

        def col(j, a):
            kj = kk_ref[pl.ds(j, 1), :]
            bj = b_ref[pl.ds(j, 1), :]
            tt = qq * kj * jnp.exp(jnp.minimum(b - bj, 0.0))
            return jnp.where(ci == j, jnp.sum(tt, axis=-1, keepdims=True), a)
        return lax.fori_loop(0, CHUNK, col, jnp.zeros((CHUNK, CHUNK), F32))

    def step_all(step, worst, exact):
        st = []
        for ch in chains:
            c = step if ch["fwd"] else n_chunks - 1 - step
            rows = pl.ds(pl.multiple_of(c * CHUNK, CHUNK), CHUNK)
            bi = ch["bi"]
            xf = ch["f"][bi, rows, ch["cols"]]
            lb = ch["lb"]
            st.append(dict(
                rows=rows, v=ch["v"][bi, rows, ch["cols"]].astype(BF16), qq=_silu(ch["q"][bi, rows, ch["cols"]]),
                kk=(1.0 - lb) * _sigmoid(-xf),
                g=jnp.log(jnp.maximum(lb + (1.0 - lb) * _sigmoid(xf), LOG_FLOOR))))
        for ch, s in zip(chains, st):
            s["b"] = _dot(ch["tri"], _stack3(s["g"]))
        for ch, s in zip(chains, st):
            fwd, b = ch["fwd"], s["b"]
            bx = b - s["g"]
            s["tot"] = b[CHUNK - 1:CHUNK] if fwd else b[0:1]
            s["refs"] = []
            for i in range(n_sub):
                lo, hi = i * SUB, (i + 1) * SUB
                r_i = bx[lo:lo + 1] if fwd else bx[hi - 1:hi]
                end = b[hi - 1:hi] if fwd else b[lo:lo + 1]
                s["refs"].append(r_i)
                worst = jnp.maximum(worst, r_i - end)
        if exact:
            for s in st:
                s["a"] = exact_scores(s["qq"], s["kk"], s["b"])
        else:
            for s in st:
                s["blocks"] = []
            for i in range(n_sub):
                lo, hi = i * SUB, (i + 1) * SUB
                for s in st:
                    qi = s["qq"][lo:hi] * jnp.exp(s["b"][lo:hi] - s["refs"][i])
                    kt = s["kk"] * jnp.exp(jnp.minimum(s["refs"][i] - s["b"], EXP_CLAMP))
                    s["blocks"].append(_dot_nt(qi.astype(BF16), kt.astype(BF16)))
            for s in st:
                s["a"] = jnp.concatenate(s["blocks"], axis=0)
        for ch, s in zip(chains, st):
            s["st"] = s_ref[ch["idx"]]
            s["o"] = _dot_nt((s["qq"] * jnp.exp(s["b"])).astype(BF16), s["st"].astype(BF16))
        for ch, s in zip(chains, st):
            a = jnp.where(ch["incl"], s["a"], 0.0)
            ch["o"][ch["bi"], s["rows"], ch["cols"]] = s["o"] + _dot(a.astype(BF16), s["v"])
        for ch, s in zip(chains, st):
            kdec = (s["kk"] * jnp.exp(s["tot"] - s["b"])).astype(BF16)
            s_ref[ch["idx"]] = jnp.exp(s["tot"]) * s["st"] + _dot_tn(s["v"], kdec)
        return worst

    def run(exact):
        return lax.fori_loop(0, n_chunks, functools.partial(step_all, exact=exact),
                             jnp.zeros((1, HEAD_DIM), F32))

    s0_ref[...] = s_ref[...]
    worst = run(False)

    @pl.when(jnp.max(worst) > EXP_CLAMP)
    def _():
        s_ref[...] = s0_ref[...]
        run(True)


def _hgrn_scan(u, lb_logits, layer, T=512, HP=4, BB=2):
    B, L, _ = u.shape
    n = L // T
    w = HP * HEAD_DIM
    qf, qb = _bidir_specs(T, n, HEAD_DIM, P_HG_Q, P_HG_Q, HP, BB)
    ff, fb = _bidir_specs(T, n, HEAD_DIM, P_HG_FF, P_HG_FB, HP, BB)
    vf, vb = _bidir_specs(T, n, HEAD_DIM, P_HG_I, P_HG_I, HP, BB)
    of, ob = _bidir_specs(T, n, HEAD_DIM, 0, 0, HP, BB)
    oshape = jax.ShapeDtypeStruct((B, L, GROUP_WIDTH), F32)
    return pl.pallas_call(
        functools.partial(_hgrn_kernel, layer=layer, T=T, HP=HP, BB=BB),
        grid=(B // BB, N_HEADS // HP, n),
        in_specs=[qf, qb, ff, fb, vf, vb,
                  pl.BlockSpec((DEPTH, 2, 1, w), lambda b, hg, t: (0, 0, 0, hg))],
        out_specs=[of, ob],
        out_shape=[oshape, oshape],
        scratch_shapes=[pltpu.VMEM((2 * HP * BB, HEAD_DIM, HEAD_DIM), F32),
                        pltpu.VMEM((2 * HP * BB, HEAD_DIM, HEAD_DIM), F32),
                        pltpu.VMEM((CHUNK, HEAD_DIM), F32),
                        pltpu.VMEM((CHUNK, HEAD_DIM), F32)],
        compiler_params=_cparams("parallel", "parallel", "arbitrary"),
        name="hgrn_scan",
    )(u, u, u, u, u, u, lb_logits.reshape(DEPTH, 2, 1, GROUP_WIDTH))


DN_HALO = 8
DN_GATE_GROUP = 4


def _dn_prep_kernel(*refs, tl):
    w_ref = refs[9]
    outs = refs[10:13]
    t = pl.program_id(1)
    has_prev = t > 0
    has_next = t < pl.num_programs(1) - 1
    ext_rows = tl + 2 * DN_HALO
    for kind in range(3):
        prev_ref, cur_ref, next_ref = refs[3 * kind:3 * kind + 3]
        for h in range(N_HEADS):
            cols = slice(h * HEAD_DIM, (h + 1) * HEAD_DIM)
            wcols = slice(kind * GROUP_WIDTH + h * HEAD_DIM, kind * GROUP_WIDTH + (h + 1) * HEAD_DIM)
            ext = jnp.concatenate([
                jnp.where(has_prev, prev_ref[0, :, cols], 0.0),
                cur_ref[0, :, cols],
                jnp.where(has_next, next_ref[0, :, cols], 0.0)], axis=0)
            acc = None
            for j in range(DN_CONV):
                tap = pltpu.roll(ext, (DN_CONV // 2 - j) % ext_rows, 0)[DN_HALO:DN_HALO + tl]
                term = tap * w_ref[j:j + 1, wcols]
                acc = term if acc is None else acc + term
            y = _silu(acc)
            if kind < 2:
                y = y * lax.rsqrt(jnp.sum(y * y, axis=-1, keepdims=True) + NORM_EPS)
            if kind == 0:
                y = y * ATTN_SCALE
            outs[kind][0, :, cols] = y


def _dn_prep(u, conv_w, tl=512):
    B, L, _ = u.shape
    nh = tl // DN_HALO
    last = L // DN_HALO - 1
    in_specs = []
    for kind in range(3):
        part = P_DN_QKV + kind
        in_specs += [
            pl.BlockSpec((1, DN_HALO, GROUP_WIDTH),
                         lambda b, t, part=part: (b, jnp.maximum(t * nh - 1, 0), part)),
            pl.BlockSpec((1, tl, GROUP_WIDTH), lambda b, t, part=part: (b, t, part)),
            pl.BlockSpec((1, DN_HALO, GROUP_WIDTH),
                         lambda b, t, part=part: (b, jnp.minimum((t + 1) * nh, last), part)),
        ]
    in_specs.append(pl.BlockSpec((DN_CONV, 3 * GROUP_WIDTH), lambda b, t: (0, 0)))
    ospec = pl.BlockSpec((1, tl, GROUP_WIDTH), lambda b, t: (b, t, 0))
    return pl.pallas_call(
        functools.partial(_dn_prep_kernel, tl=tl),
        grid=(B, L // tl),
        in_specs=in_specs,
        out_specs=[ospec, ospec, ospec],
        out_shape=[jax.ShapeDtypeStruct((B, L, GROUP_WIDTH), F32)] * 3,
        compiler_params=_cparams("parallel", "parallel"),
        name="dn_prep",
    )(*([u] * 9), conv_w)


def _softplus(x):
    return jnp.maximum(x, 0.0) + jnp.log1p(jnp.exp(-jnp.abs(x)))


def _dn_gates_kernel(ab_ref, alog_ref, dtb_ref, cols_ref, rows_ref, *, T):
    lane = _iota2((1, HEAD_DIM), 1)
    neg_a = -jnp.exp(alog_ref[...])
    tri_f, tri_b, eye4 = _tri4(True), _tri4(False), _eye4()

    def body(it, carry):
        st = []
        for j in range(DN_GATE_GROUP):
            c = it * DN_GATE_GROUP + j
            rows = pl.ds(pl.multiple_of(c * CHUNK, CHUNK), CHUNK)
            ab = ab_ref[0, rows, :]
            st.append(dict(c=c, rows=rows, ab=ab, g3=_stack3(neg_a * _softplus(ab + dtb_ref[...]))))
        for s in st:
            s["G"] = jnp.where(lane < N_HEADS, _dot(tri_f, s["g3"]), _dot(tri_b, s["g3"]))
        for s in st:
            cols_ref[0, s["rows"], :] = jnp.where(lane < 2 * N_HEADS, s["G"], _sigmoid(s["ab"]))
            rows_ref[0, s["c"]] = _dot_tn(_stack3(s["G"]), eye4)[0:4 * N_HEADS]
        return carry
    lax.fori_loop(0, T // CHUNK // DN_GATE_GROUP, body, 0)


def _dn_gates(uab, a_log, dt_bias, T=1024):
    B, L, _ = uab.shape
    pad = HEAD_DIM - 2 * N_HEADS
    alog = jnp.pad(a_log.astype(F32).reshape(1, 2 * N_HEADS), ((0, 0), (0, pad)))
    dtb = jnp.pad(dt_bias.astype(F32).reshape(1, 2 * N_HEADS), ((0, 0), (0, pad)))
    pspec = pl.BlockSpec((1, HEAD_DIM), lambda b, t: (0, 0))
    return pl.pallas_call(
        functools.partial(_dn_gates_kernel, T=T),
        grid=(B, L // T),
        in_specs=[pl.BlockSpec((1, T, HEAD_DIM), lambda b, t: (b, t, 0)), pspec, pspec],
        out_specs=[pl.BlockSpec((1, T, HEAD_DIM), lambda b, t: (b, t, 0)),
                   pl.BlockSpec((1, T // CHUNK, 4 * N_HEADS, HEAD_DIM), lambda b, t: (b, t, 0, 0))],
        out_shape=[jax.ShapeDtypeStruct((B, L, HEAD_DIM), F32),
                   jax.ShapeDtypeStruct((B, L // CHUNK, 4 * N_HEADS, HEAD_DIM), F32)],
        compiler_params=_cparams("parallel", "parallel"),
        name="dn_gates",
    )(uab, alog, dtb)


def _lhs3(x):
    hi, lo = _split2(x)
    return jnp.concatenate([hi, lo], axis=1)


def _rhs3(x):
    hi, lo = _split2(x)
    return jnp.concatenate([hi, lo, hi, jnp.zeros_like(hi)], axis=0)


def _dn_kernel(qf_ref, qb_ref, kf_ref, kb_ref, vf_ref, vb_ref, gcf_ref, gcb_ref, grf_ref, grb_ref,
               of_ref, ob_ref, s_ref, *, T, HP, BB):
    hg = pl.program_id(1)

    @pl.when(pl.program_id(2) == 0)
    def _():
        s_ref[...] = jnp.zeros_like(s_ref)

    n_chunks = T // CHUNK
    lane = _iota2((1, HEAD_DIM), 1)
    eye = jnp.where(_iota2((CHUNK, HEAD_DIM), 0) == (_iota2((CHUNK, HEAD_DIM), 1) & (CHUNK - 1)), 1.0, 0.0)

    chains = []
    for bi in range(BB):
        for hh in range(HP):
            for d in range(2):
                fwd = d == 0
                incl, strict = _order_masks(fwd, HEAD_DIM)
                chains.append(dict(
                    idx=(bi * HP + hh) * 2 + d, bi=bi, cols=slice(hh * HEAD_DIM, (hh + 1) * HEAD_DIM), fwd=fwd,
                    col=d * N_HEADS + hg * HP + hh, incl=incl, strict=strict,
                    q=(qf_ref, qb_ref)[d], k=(kf_ref, kb_ref)[d], v=(vf_ref, vb_ref)[d],
                    gc=(gcf_ref, gcb_ref)[d], gr=(grf_ref, grb_ref)[d], o=(of_ref, ob_ref)[d]))

    def body(step, carry):
        st = []
        for ch in chains:
            cols, col, bi = ch["cols"], ch["col"], ch["bi"]
            c = step if ch["fwd"] else n_chunks - 1 - step
            rows = pl.ds(pl.multiple_of(c * CHUNK, CHUNK), CHUNK)
            gates = ch["gc"][bi, rows, :]
            k = ch["k"][bi, rows, cols]
            kh = k.astype(BF16)
            st.append(dict(
                rows=rows, q=ch["q"][bi, rows, cols], k=k, v=ch["v"][bi, rows, cols],
                k2=jnp.concatenate([kh, kh], axis=0),
                g_col=jnp.sum(jnp.where(lane == col, gates, 0.0), axis=-1, keepdims=True),
                beta=jnp.sum(jnp.where(lane == col + 2 * N_HEADS, gates, 0.0), axis=-1, keepdims=True),
                g_row=ch["gr"][bi, c, pl.ds(col, 1), :]))
        for ch, s in zip(chains, st):
            s["gamma"] = jnp.where(ch["incl"], jnp.exp(jnp.minimum(s["g_col"] - s["g_row"], 0.0)), 0.0)
            s["kb"] = s["k"] * s["beta"]
        for ch, s in zip(chains, st):
            n_mat = jnp.where(ch["strict"], _dot_nt(s["kb"].astype(BF16), s["k2"]) * s["gamma"], 0.0)
            s["x"] = eye - n_mat
            s["nl"], s["nr"] = _lhs3(n_mat), _rhs3(n_mat)

        order = 1
        while True:
            for s in st:
                s["nk"] = _dot(s["nl"], s["nr"])
            for s in st:
                s["nr"] = _rhs3(s["nk"])
                s["x"] = s["x"] + _dot(_lhs3(s["x"]), s["nr"])
            order *= 2
            if 2 * order >= CHUNK:
                break
            for s in st:
                s["nl"] = _lhs3(s["nk"])

        for s in st:
            s["eg"] = jnp.exp(s["g_col"])
            rhs = jnp.concatenate([s["kb"] * s["eg"], s["v"] * s["beta"]], axis=1)
            s["wu"] = _dot(_lhs3(s["x"]), _rhs3(rhs))
        for ch, s in zip(chains, st):
            s["a_qk"] = (_dot_nt(s["q"].astype(BF16), s["k2"]) * s["gamma"])[:, 0:CHUNK].astype(BF16)
            s["s"] = s_ref[ch["idx"]]
            s["sb"] = s["s"].astype(BF16)
        for s in st:
            s["v_new"] = (s["wu"][:, HEAD_DIM:] - _dot(s["wu"][:, :HEAD_DIM].astype(BF16), s["sb"])).astype(BF16)
        for ch, s in zip(chains, st):
            ch["o"][ch["bi"], s["rows"], ch["cols"]] = (_dot((s["q"] * s["eg"]).astype(BF16), s["sb"])
                                                 + _dot(s["a_qk"], s["v_new"]))
        for ch, s in zip(chains, st):
            g_last = s["g_col"][CHUNK - 1:CHUNK] if ch["fwd"] else s["g_col"][0:1]
            k_dec = (s["k"] * jnp.exp(g_last - s["g_col"])).astype(BF16)
            s_ref[ch["idx"]] = jnp.exp(g_last) * s["s"] + _dot_tn(k_dec, s["v_new"])
        return carry
    lax.fori_loop(0, n_chunks, body, 0)


def _dn_scan(qh, kh, vh, gate_cols, gate_rows, T=512, HP=4, BB=2):
    B, L, _ = qh.shape
    n = L // T
    hf, hb = _bidir_specs(T, n, HEAD_DIM, 0, 0, HP, BB)
    gcf = pl.BlockSpec((BB, T, HEAD_DIM), lambda b, hg, t: (b, t, 0))
    gcb = pl.BlockSpec((BB, T, HEAD_DIM), lambda b, hg, t: (b, n - 1 - t, 0))
    rshape = (BB, T // CHUNK, 4 * N_HEADS, HEAD_DIM)
    grf = pl.BlockSpec(rshape, lambda b, hg, t: (b, t, 0, 0))
    grb = pl.BlockSpec(rshape, lambda b, hg, t: (b, n - 1 - t, 0, 0))
    oshape = jax.ShapeDtypeStruct((B, L, GROUP_WIDTH), F32)
    return pl.pallas_call(
        functools.partial(_dn_kernel, T=T, HP=HP, BB=BB),
        grid=(B // BB, N_HEADS // HP, n),
        in_specs=[hf, hb, hf, hb, hf, hb, gcf, gcb, grf, grb],
        out_specs=[hf, hb],
        out_shape=[oshape, oshape],
        scratch_shapes=[pltpu.VMEM((2 * HP * BB, HEAD_DIM, HEAD_DIM), F32)],
        compiler_params=_cparams("parallel", "parallel", "arbitrary"),
        name="dn_scan",
    )(qh, qh, kh, kh, vh, vh, gate_cols, gate_cols, gate_rows, gate_rows)


def _encoder_layer(x, mod, layer, p, rope_tabs):
    u, uab = _in_proj(x, p["norm1_g"][layer], mod, p["w_in_main"], p["w_in_ab"][layer], layer)

    swa_q, swa_k = _qk_prep(u, P_SWA_Q, P_SWA_K, None, p["swa_q_norm"][layer], p["swa_k_norm"][layer],
                            rope_tabs, F32)
    y_swa = _swa_attention(swa_q, swa_k, u)

    o_hg = _hgrn_scan(u, p["hgrn_lb_logits"], layer)

    na_q, na_k, na_v = _qk_prep(u, P_NA_Q, P_NA_K, P_NA_V, p["na_q_norm"][layer], p["na_k_norm"][layer],
                                None, BF16)
    y_na = _na_attention(na_q, na_k, na_v, p["na_bias"][layer])

    dq, dk, dv = _dn_prep(u, p["dn_conv_w"][layer])
    gate_cols, gate_rows = _dn_gates(uab, p["dn_a_log"][layer], p["dn_dt_bias"][layer])
    o_dn = _dn_scan(dq, dk, dv, gate_cols, gate_rows)

    x = _out_proj(x, mod, u, y_swa, y_na, o_hg, p["hgrn_norm_g"][layer], o_dn, p["dn_norm_g"][layer],
                  p["w_out"], layer)
    return _mlp(x, p["norm2_g"][layer], mod, p["w_mlp_in"], p["w_mlp_out"], layer)


def _run_trunk(x, mod_all, p):
    B, L, _ = x.shape
    rope_tabs = _rope_tables(L)
    for layer in range(DEPTH):
        x = _encoder_layer(x, mod_all[layer].reshape(B, 6, D_MODEL), layer, p, rope_tabs)
    return x


def kernel(x_prompt, x_sample, c_prompt, c_sample, norm1_g, norm2_g, ada_w, ada_b, w_in, w_out,
           swa_q_norm, swa_k_norm, hgrn_lb_logits, hgrn_norm_g, na_q_norm, na_k_norm, na_rpb,
           dn_conv_w, dn_a_log, dn_dt_bias, dn_norm_g, w_mlp_in, w_mlp_out):
    bp, bs = c_prompt.shape[0], c_sample.shape[0]
    pad_rows = -(bp + bs) % 8
    c_all = jnp.concatenate([c_prompt, c_sample, jnp.zeros((pad_rows, D_MODEL), F32)], axis=0)
    mod_all = _modulation(c_all, ada_w, ada_b)

    p = dict(
        norm1_g=norm1_g, norm2_g=norm2_g,
        w_in_main=w_in.astype(BF16),
        w_in_ab=jnp.pad(w_in[:, :, IN_MAIN:], ((0, 0), (0, 0), (0, HEAD_DIM - 4 * N_HEADS))).astype(BF16),
        w_out=w_out.astype(BF16),
        swa_q_norm=swa_q_norm, swa_k_norm=swa_k_norm,
        hgrn_lb_logits=hgrn_lb_logits.astype(F32), hgrn_norm_g=hgrn_norm_g,
        na_q_norm=na_q_norm, na_k_norm=na_k_norm,
        na_bias=jnp.stack([_na_bias_table(na_rpb[l]) for l in range(DEPTH)]),
        dn_conv_w=dn_conv_w, dn_a_log=dn_a_log, dn_dt_bias=dn_dt_bias, dn_norm_g=dn_norm_g,
        w_mlp_in=w_mlp_in.astype(BF16), w_mlp_out=w_mlp_out.astype(BF16),
    )
    y_prompt = _run_trunk(x_prompt, mod_all[:, :bp], p)
    y_sample = _run_trunk(x_sample, mod_all[:, bp:bp + bs], p)
    return (y_prompt, y_sample)
```

```python
import functools
import math

import jax
import jax.numpy as jnp
from jax import lax
from jax.experimental import pallas as pl
from jax.experimental.pallas import tpu as pltpu

F32 = jnp.float32
BF16 = jnp.bfloat16

D_MODEL = 2048
DEPTH = 4
HEAD_DIM = 128
N_HEADS = 4
GROUP_WIDTH = N_HEADS * HEAD_DIM
IN_MAIN = 15 * GROUP_WIDTH
D_FF = 4 * D_MODEL
NORM_EPS = 1e-6
ROPE_THETA = 500000.0
ROPE_DIM = HEAD_DIM // 4
DILATED_CONFIGS = ((128, 1), (512, 4), (2048, 16))
GRID_W = 64
NA_KH = 8
NA_KW = 16
CHUNK = 64
SUB = 16
DN_CONV = 5
MASK_VALUE = -1e30
LOG_FLOOR = 1e-30
ATTN_SCALE = HEAD_DIM ** -0.5
EXP_CLAMP = 60.0
HIGHEST = lax.Precision.HIGHEST

P_SWA_Q, P_SWA_K, P_SWA_V = 0, 1, 2
P_HG_Q, P_HG_FF, P_HG_FB, P_HG_I, P_HG_G = 3, 4, 5, 6, 7
P_NA_Q, P_NA_K, P_NA_V = 8, 9, 10
P_DN_QKV, P_DN_Z = 11, 14

VMEM_LIMIT = 56 * 1024 * 1024


def _cparams(*sem):
    return pltpu.CompilerParams(dimension_semantics=sem, vmem_limit_bytes=VMEM_LIMIT)


def _dot(a, b):
    return jnp.dot(a, b, preferred_element_type=F32)


def _dot_nt(a, b):
    return lax.dot_general(a, b, (((1,), (1,)), ((), ())), preferred_element_type=F32)


def _dot_tn(a, b):
    return lax.dot_general(a, b, (((0,), (0,)), ((), ())), preferred_element_type=F32)


def _sigmoid(x):
    return jax.nn.sigmoid(x)


def _silu(x):
    return x * _sigmoid(x)


def _rms(x):
    return x * lax.rsqrt(jnp.mean(x * x, axis=-1, keepdims=True) + NORM_EPS)


def _split2(x):
    hi = x.astype(BF16)
    return hi, (x - hi.astype(F32)).astype(BF16)


def _split3(x):
    hi = x.astype(BF16)
    r1 = x - hi.astype(F32)
    mid = r1.astype(BF16)
    return hi, mid, (r1 - mid.astype(F32)).astype(BF16)


def _stack3(x):
    hi, mid, lo = _split3(x)
    return jnp.concatenate([hi, mid, lo, jnp.zeros_like(hi)], axis=0)


def _iota2(shape, axis):
    return lax.broadcasted_iota(jnp.int32, shape, axis)


def _tri4(fwd):
    ri, ci = _iota2((CHUNK, 4 * CHUNK), 0), _iota2((CHUNK, 4 * CHUNK), 1)
    cm = ci & (CHUNK - 1)
    order = (cm <= ri) if fwd else (cm >= ri)
    return jnp.where(order & (ci < 3 * CHUNK), 1.0, 0.0).astype(BF16)


def _eye4():
    ri, ci = _iota2((4 * CHUNK, 2 * CHUNK), 0), _iota2((4 * CHUNK, 2 * CHUNK), 1)
    same = (ri & (CHUNK - 1)) == (ci & (CHUNK - 1))
    return jnp.where(same & (ri < 3 * CHUNK), 1.0, 0.0).astype(BF16)


def _order_masks(fwd, width):
    ri, ci = _iota2((CHUNK, width), 0), _iota2((CHUNK, width), 1) & (CHUNK - 1)
    if fwd:
        return ci <= ri, ci < ri
    return ci >= ri, ci > ri


def _mod_kernel(c_ref, w_ref, b_ref, o_ref):
    a = _silu(c_ref[...]).astype(BF16)
    o_ref[0] = _dot(a, w_ref[0].astype(BF16)) + b_ref[0]


def _modulation(c_all, ada_w, ada_b):
    rows, d = c_all.shape
    n = ada_w.shape[-1]
    tn = 1024
    return pl.pallas_call(
        _mod_kernel,
        grid=(DEPTH, n // tn),
        in_specs=[
            pl.BlockSpec((rows, d), lambda l, j: (0, 0)),
            pl.BlockSpec((1, d, tn), lambda l, j: (l, 0, j)),
            pl.BlockSpec((1, 1, tn), lambda l, j: (l, 0, j)),
        ],
        out_specs=pl.BlockSpec((1, rows, tn), lambda l, j: (l, 0, j)),
        out_shape=jax.ShapeDtypeStruct((DEPTH, rows, n), F32),
        compiler_params=_cparams("parallel", "parallel"),
        name="adaln_mod",
    )(c_all, ada_w, ada_b.reshape(DEPTH, 1, n))


NORM_ROWS = 128


def _norm_mod_store(h_ref, x_ref, g, scale, shift, tm):
    gain = g * (1.0 + scale)

    def body(c, carry):
        rows = pl.ds(pl.multiple_of(c * NORM_ROWS, NORM_ROWS), NORM_ROWS)
        x = x_ref[0, rows, :]
        inv = lax.rsqrt(jnp.mean(x * x, axis=-1, keepdims=True) + NORM_EPS)
        h_ref[rows, :] = (x_ref[0, rows, :] * inv * gain + shift).astype(BF16)
        return carry
    lax.fori_loop(0, tm // NORM_ROWS, body, 0)


def _in_proj_kernel(x_ref, g_ref, mod_ref, w_ref, wab_ref, u_ref, uab_ref, h_ref, *, tm):
    @pl.when(pl.program_id(2) == 0)
    def _():
        _norm_mod_store(h_ref, x_ref, g_ref[...], mod_ref[0, 1:2, :], mod_ref[0, 0:1, :], tm)
        uab_ref[0] = _dot(h_ref[...], wab_ref[...])
    u_ref[0] = _dot(h_ref[...], w_ref[...])


def _in_proj(x, g, mod, w_main, w_ab, layer, tm=512, tn=1536):
    B, L, D = x.shape
    N = IN_MAIN
    return pl.pallas_call(
        functools.partial(_in_proj_kernel, tm=tm),
        grid=(B, L // tm, N // tn),
        in_specs=[
            pl.BlockSpec((1, tm, D), lambda b, i, j: (b, i, 0)),
            pl.BlockSpec((1, D), lambda b, i, j: (0, 0)),
            pl.BlockSpec((1, 6, D), lambda b, i, j: (b, 0, 0)),
            pl.BlockSpec((None, D, tn), lambda b, i, j: (layer, 0, j)),
            pl.BlockSpec((D, HEAD_DIM), lambda b, i, j: (0, 0)),
        ],
        out_specs=[
            pl.BlockSpec((1, tm, tn), lambda b, i, j: (b, i, j)),
            pl.BlockSpec((1, tm, HEAD_DIM), lambda b, i, j: (b, i, 0)),
        ],
        out_shape=[jax.ShapeDtypeStruct((B, L, N), F32),
                   jax.ShapeDtypeStruct((B, L, HEAD_DIM), F32)],
        scratch_shapes=[pltpu.VMEM((tm, D), BF16)],
        compiler_params=_cparams("parallel", "parallel", "arbitrary"),
        name="in_proj",
    )(x, g.reshape(1, D), mod, w_main, w_ab)


def _gated_heads(of_ref, ob_ref, z_ref, g_ref):
    parts = []
    for h in range(N_HEADS):
        cols = slice(h * HEAD_DIM, (h + 1) * HEAD_DIM)
        o = of_ref[0, :, cols] + ob_ref[0, :, cols]
        parts.append((_rms(o) * g_ref[...] * _silu(z_ref[0, :, cols])).astype(BF16))
    return jnp.concatenate(parts, axis=1)


def _out_proj_kernel(x_ref, mod_ref, swa_ref, na_ref, hgf_ref, hgb_ref, hgz_ref, hgg_ref,
                     dnf_ref, dnb_ref, dnz_ref, dng_ref, w_ref, o_ref):
    ys = (swa_ref[0], _gated_heads(hgf_ref, hgb_ref, hgz_ref, hgg_ref),
          na_ref[0], _gated_heads(dnf_ref, dnb_ref, dnz_ref, dng_ref))
    acc = None
    for m, y in enumerate(ys):
        part = _dot(y, w_ref[m * GROUP_WIDTH:(m + 1) * GROUP_WIDTH, :])
        acc = part if acc is None else acc + part
    o_ref[0] = x_ref[0] + mod_ref[0, 2:3, :] * acc


def _out_proj(x, mod, u, y_swa, y_na, o_hg, g_hg, o_dn, g_dn, w_out, layer, tm=512):
    B, L, D = x.shape
    yspec = pl.BlockSpec((1, tm, GROUP_WIDTH), lambda b, i: (b, i, 0))
    gspec = pl.BlockSpec((1, HEAD_DIM), lambda b, i: (0, 0))
    zspec = lambda part: pl.BlockSpec((1, tm, GROUP_WIDTH), lambda b, i: (b, i, part))
    return pl.pallas_call(
        _out_proj_kernel,
        grid=(B, L // tm),
        in_specs=[
            pl.BlockSpec((1, tm, D), lambda b, i: (b, i, 0)),
            pl.BlockSpec((1, 6, D), lambda b, i: (b, 0, 0)),
            yspec, yspec,
            yspec, yspec, zspec(P_HG_G), gspec,
            yspec, yspec, zspec(P_DN_Z), gspec,
            pl.BlockSpec((None, D, D), lambda b, i: (layer, 0, 0)),
        ],
        out_specs=pl.BlockSpec((1, tm, D), lambda b, i: (b, i, 0)),
        out_shape=jax.ShapeDtypeStruct((B, L, D), F32),
        compiler_params=_cparams("parallel", "parallel"),
        name="out_proj",
    )(x, mod, y_swa, y_na, *o_hg, u, g_hg.reshape(1, HEAD_DIM), *o_dn, u, g_dn.reshape(1, HEAD_DIM), w_out)


def _mlp_kernel(x_ref, g_ref, mod_ref, w1_ref, w2_ref, o_ref, h_ref, *, tm):
    j = pl.program_id(2)

    @pl.when(j == 0)
    def _():
        _norm_mod_store(h_ref, x_ref, g_ref[...], mod_ref[0, 4:5, :], mod_ref[0, 3:4, :], tm)
        o_ref[...] = jnp.zeros_like(o_ref)

    half = w1_ref.shape[1] // 2
    hids = [jnp.square(jnp.maximum(_dot(h_ref[...], w1_ref[:, c * half:(c + 1) * half]), 0.0)).astype(BF16)
            for c in range(2)]
    o_ref[0] += _dot(hids[0], w2_ref[0:half, :]) + _dot(hids[1], w2_ref[half:, :])

    @pl.when(j == pl.num_programs(2) - 1)
    def _():
        o_ref[0] = x_ref[0] + mod_ref[0, 5:6, :] * o_ref[0]


def _mlp(x, g, mod, w1, w2, layer, tm=512, tf=1024):
    B, L, D = x.shape
    F = w1.shape[-1]
    return pl.pallas_call(
        functools.partial(_mlp_kernel, tm=tm),
        grid=(B, L // tm, F // tf),
        in_specs=[
            pl.BlockSpec((1, tm, D), lambda b, i, j: (b, i, 0)),
            pl.BlockSpec((1, D), lambda b, i, j: (0, 0)),
            pl.BlockSpec((1, 6, D), lambda b, i, j: (b, 0, 0)),
            pl.BlockSpec((None, D, tf), lambda b, i, j: (layer, 0, j)),
            pl.BlockSpec((None, tf, D), lambda b, i, j: (layer, j, 0)),
        ],
        out_specs=pl.BlockSpec((1, tm, D), lambda b, i, j: (b, i, 0)),
        out_shape=jax.ShapeDtypeStruct((B, L, D), F32),
        scratch_shapes=[pltpu.VMEM((tm, D), BF16)],
        compiler_params=_cparams("parallel", "parallel", "arbitrary"),
        name="mlp",
    )(x, g.reshape(1, D), mod, w1, w2)


def _qk_prep_kernel(*refs, rope):
    if rope:
        q_ref, k_ref, gq_ref, gk_ref, c_ref, s1_ref, s2_ref, qo_ref, ko_ref = refs
    else:
        q_ref, k_ref, v_ref, gq_ref, gk_ref, qo_ref, ko_ref, vo_ref = refs
        vo_ref[...] = v_ref[...].astype(vo_ref.dtype)
    ones = jnp.ones((HEAD_DIM, HEAD_DIM), BF16)
    for src, g_ref, dst in ((q_ref, gq_ref, qo_ref), (k_ref, gk_ref, ko_ref)):
        for h in range(N_HEADS):
            cols = slice(h * HEAD_DIM, (h + 1) * HEAD_DIM)
            x = src[0, :, cols]
            hi, lo = _split2(x * x)
            ss = _dot(hi, ones) + _dot(lo, ones)
            y = x * lax.rsqrt(ss * (1.0 / HEAD_DIM) + NORM_EPS) * g_ref[...]
            if rope:
                y = (y * c_ref[...] + pltpu.roll(y, ROPE_DIM // 2, 1) * s1_ref[...]
                     + pltpu.roll(y, HEAD_DIM - ROPE_DIM // 2, 1) * s2_ref[...])
            dst[0, :, cols] = y.astype(dst.dtype)


def _rope_tables(L):
    half = ROPE_DIM // 2
    inv_freq = jnp.power(ROPE_THETA, -jnp.arange(half, dtype=F32) / half)
    ang = jnp.arange(L).astype(F32)[:, None] * inv_freq[None, :]
    cos, sin = jnp.cos(ang), jnp.sin(ang)
    zeros = jnp.zeros((L, HEAD_DIM - ROPE_DIM), F32)
    zh = jnp.zeros((L, half), F32)
    c = jnp.concatenate([cos, cos, jnp.ones((L, HEAD_DIM - ROPE_DIM), F32)], axis=1)
    s1 = jnp.concatenate([zh, sin, zeros], axis=1)
    s2 = jnp.concatenate([-sin, zh, zeros], axis=1)
    return c, s1, s2


def _qk_prep(u, part_q, part_k, part_v, gq, gk, rope_tabs, out_dtype, tl=512):
    B, L, _ = u.shape
    rope = rope_tabs is not None
    uspec = lambda part: pl.BlockSpec((1, tl, GROUP_WIDTH), lambda b, t, part=part: (b, t, part))
    gspec = pl.BlockSpec((1, HEAD_DIM), lambda b, t: (0, 0))
    tspec = pl.BlockSpec((tl, HEAD_DIM), lambda b, t: (t, 0))
    ospec = pl.BlockSpec((1, tl, GROUP_WIDTH), lambda b, t: (b, t, 0))
    gains = [gq.reshape(1, HEAD_DIM), gk.reshape(1, HEAD_DIM)]
    if rope:
        in_specs = [uspec(part_q), uspec(part_k), gspec, gspec, tspec, tspec, tspec]
        args = [u, u] + gains + list(rope_tabs)
    else:
        in_specs = [uspec(part_q), uspec(part_k), uspec(part_v), gspec, gspec]
        args = [u, u, u] + gains
    n_out = 2 if rope else 3
    return pl.pallas_call(
        functools.partial(_qk_prep_kernel, rope=rope),
        grid=(B, L // tl),
        in_specs=in_specs,
        out_specs=[ospec] * n_out,
        out_shape=[jax.ShapeDtypeStruct((B, L, GROUP_WIDTH), out_dtype)] * n_out,
        compiler_params=_cparams("parallel", "parallel"),
        name="qk_prep_rope" if rope else "qkv_prep",
    )(*args)


SWA_Q = 128
SWA_K = 2 * SWA_Q
SWA_RADIUS = 64
SWA_ROWS = 256
SWA_GROUP = 8


def _swa_kernel(q_ref, k_ref, v_ref, o_ref, ob_ref, ls_ref, mask_ref, *, L, T):
    t0 = pl.program_id(2) * T
    diff = _iota2((SWA_Q, SWA_K), 1) - _iota2((SWA_Q, SWA_K), 0)
    for i in range(3):
        mask_ref[i] = jnp.where(jnp.abs(diff - i * SWA_RADIUS) <= SWA_RADIUS, 0.0, MASK_VALUE)

    for bi, (window, dil) in enumerate(DILATED_CONFIGS):
        assert window // (2 * dil) == SWA_RADIUS
        span = SWA_Q * dil
        n_sub = T // span

        def body(it, carry, dil=dil, bi=bi, span=span, n_sub=n_sub):
            st = []
            for j in range(SWA_GROUP):
                idx = it * SWA_GROUP + j
                r, loc = idx // n_sub, (idx % n_sub) * span
                base = jnp.clip(t0 + loc - SWA_RADIUS * dil, 0, L - SWA_K * dil)
                if dil == 1:
                    qs = pl.ds(pl.multiple_of(loc, SWA_Q), SWA_Q)
                    ks = pl.ds(pl.multiple_of(base, SWA_RADIUS), SWA_K)
                else:
                    qs = pl.ds(loc + r, SWA_Q, stride=dil)
                    ks = pl.ds(base + r, SWA_K, stride=dil)
                st.append(dict(qs=qs, ks=ks, mask=(t0 + loc - base) // (dil * SWA_RADIUS)))
            for s in st:
                sc = _dot_nt(q_ref[0, s["qs"], :].astype(BF16), k_ref[0, s["ks"], :].astype(BF16))
                s["s"] = sc * ATTN_SCALE + mask_ref[s["mask"]]
            for s in st:
                s["m"] = jnp.max(s["s"], axis=-1, keepdims=True)
                s["e"] = jnp.exp(s["s"] - s["m"])
                s["den"] = jnp.sum(s["e"], axis=-1, keepdims=True)
            for s in st:
                pv = _dot(s["e"].astype(BF16), v_ref[0, s["ks"], :].astype(BF16))
                ob_ref[bi, s["qs"], :] = pv / s["den"]
                ls_ref[bi, s["qs"], :] = jnp.broadcast_to(s["m"] + jnp.log(s["den"]), (SWA_Q, HEAD_DIM))
            return carry
        lax.fori_loop(0, T // SWA_Q // SWA_GROUP, body, 0)

    def combine(c, carry):
        rows = pl.ds(pl.multiple_of(c * SWA_ROWS, SWA_ROWS), SWA_ROWS)
        l0, l1, l2 = ls_ref[0, rows, :], ls_ref[1, rows, :], ls_ref[2, rows, :]
        mx = jnp.maximum(l0, jnp.maximum(l1, l2))
        w0, w1, w2 = jnp.exp(l0 - mx), jnp.exp(l1 - mx), jnp.exp(l2 - mx)
        num = w0 * ob_ref[0, rows, :] + w1 * ob_ref[1, rows, :] + w2 * ob_ref[2, rows, :]
        o_ref[0, rows, :] = (num / (w0 + w1 + w2)).astype(o_ref.dtype)
        return carry
    lax.fori_loop(0, T // SWA_ROWS, combine, 0)


def _swa_attention(qn, kn, u, T=2048):
    B, L, _ = qn.shape
    max_dil = max(d for _, d in DILATED_CONFIGS)
    assert L % T == 0 and T % (SWA_Q * max_dil) == 0 and L >= SWA_K * max_dil
    nb = len(DILATED_CONFIGS)
    return pl.pallas_call(
        functools.partial(_swa_kernel, L=L, T=T),
        grid=(B, N_HEADS, L // T),
        in_specs=[
            pl.BlockSpec((1, T, HEAD_DIM), lambda b, h, t: (b, t, h)),
            pl.BlockSpec((1, L, HEAD_DIM), lambda b, h, t: (b, 0, h)),
            pl.BlockSpec((1, L, HEAD_DIM), lambda b, h, t: (b, 0, P_SWA_V * N_HEADS + h)),
        ],
        out_specs=pl.BlockSpec((1, T, HEAD_DIM), lambda b, h, t: (b, t, h)),
        out_shape=jax.ShapeDtypeStruct((B, L, GROUP_WIDTH), BF16),
        scratch_shapes=[pltpu.VMEM((nb, T, HEAD_DIM), F32), pltpu.VMEM((nb, T, HEAD_DIM), F32),
                        pltpu.VMEM((3, SWA_Q, SWA_K), F32)],
        compiler_params=_cparams("parallel", "parallel", "arbitrary"),
        name="swa_attention",
    )(qn, kn, u)


NA_KEYS = NA_KH * GRID_W
NA_GROUP = 8


def _na_bias_table(rpb):
    cq = jnp.arange(GRID_W)
    ck = jnp.arange(GRID_W)
    c0 = jnp.clip(cq - NA_KW // 2, 0, GRID_W - NA_KW)
    col_ok = (ck[None, :] >= c0[:, None]) & (ck[None, :] < c0[:, None] + NA_KW)
    col_off = jnp.clip(ck[None, :] - cq[:, None], -(NA_KW - 1), NA_KW - 1) + (NA_KW - 1)
    onehot = (col_off[None] == jnp.arange(2 * NA_KW - 1)[:, None, None]).astype(F32)
    cols = jnp.einsum('hrc,cqw->hrqw', rpb.astype(F32), onehot, precision=lax.Precision.HIGHEST)
    cols = jnp.where(col_ok[None, None], cols, MASK_VALUE)
    bias = jnp.stack([cols[:, s:s + NA_KH] for s in range(NA_KH)], axis=1)
    bias = bias.transpose(0, 1, 3, 2, 4)
    return bias.reshape(rpb.shape[0], NA_KH, GRID_W, NA_KEYS)


def _na_kernel(q_ref, k_ref, v_ref, bias_ref, o_ref, *, rows, R):
    t = pl.program_id(2)

    def body(it, carry):
        st = []
        for j in range(NA_GROUP):
            i = it * NA_GROUP + j
            r = t * R + i
            r0 = jnp.clip(r - NA_KH // 2, 0, rows - NA_KH)
            st.append(dict(shift=r0 - r + (NA_KH - 1),
                           qs=pl.ds(pl.multiple_of(i * GRID_W, GRID_W), GRID_W),
                           ks=pl.ds(pl.multiple_of(r0 * GRID_W, GRID_W), NA_KEYS)))
        for s in st:
            s["s"] = _dot_nt(q_ref[0, s["qs"], :], k_ref[0, s["ks"], :]) * ATTN_SCALE + bias_ref[0, s["shift"]]
        for s in st:
            s["m"] = jnp.max(s["s"], axis=-1, keepdims=True)
            s["e"] = jnp.exp(s["s"] - s["m"])
            s["den"] = jnp.sum(s["e"], axis=-1, keepdims=True)
        for s in st:
            o_ref[0, s["qs"], :] = (_dot(s["e"].astype(BF16), v_ref[0, s["ks"], :]) / s["den"]).astype(o_ref.dtype)
        return carry
    lax.fori_loop(0, R // NA_GROUP, body, 0)


def _na_attention(qn, kn, vn, bias, R=16):
    B, L, _ = qn.shape
    rows = L // GRID_W
    assert rows >= NA_KH and rows % R == 0
    T = R * GRID_W
    return pl.pallas_call(
        functools.partial(_na_kernel, rows=rows, R=R),
        grid=(B, N_HEADS, rows // R),
        in_specs=[
            pl.BlockSpec((1, T, HEAD_DIM), lambda b, h, t: (b, t, h)),
            pl.BlockSpec((1, L, HEAD_DIM), lambda b, h, t: (b, 0, h)),
            pl.BlockSpec((1, L, HEAD_DIM), lambda b, h, t: (b, 0, h)),
            pl.BlockSpec((1, NA_KH, GRID_W, NA_KEYS), lambda b, h, t: (h, 0, 0, 0)),
        ],
        out_specs=pl.BlockSpec((1, T, HEAD_DIM), lambda b, h, t: (b, t, h)),
        out_shape=jax.ShapeDtypeStruct((B, L, GROUP_WIDTH), BF16),
        compiler_params=_cparams("parallel", "parallel", "arbitrary"),
        name="na_attention",
    )(qn, kn, vn, bias)


def _bidir_specs(T, n, width, part_f, part_b, heads_per_step, batch_per_step=1):
    w = heads_per_step * width
    bb = batch_per_step
    return (pl.BlockSpec((bb, T, w), lambda b, hg, t: (b, t, part_f * (GROUP_WIDTH // w) + hg)),
            pl.BlockSpec((bb, T, w), lambda b, hg, t: (b, n - 1 - t, part_b * (GROUP_WIDTH // w) + hg)))


def _hgrn_kernel(qf_ref, qb_ref, ff_ref, fb_ref, vf_ref, vb_ref, lg_ref, of_ref, ob_ref,
                 s_ref, s0_ref, kk_ref, b_ref, *, layer, T, HP, BB):
    @pl.when(pl.program_id(2) == 0)
    def _():
        s_ref[...] = jnp.zeros_like(s_ref)

    n_chunks = T // CHUNK
    n_sub = CHUNK // SUB
    ci = _iota2((CHUNK, CHUNK), 1)

    chains = []
    for hh in range(HP):
        cols = slice(hh * HEAD_DIM, (hh + 1) * HEAD_DIM)
        for d in range(2):
            lg = lg_ref[:, d, 0, cols]
            ex = jnp.exp(lg - jnp.max(lg, axis=0, keepdims=True))
            p = ex / jnp.sum(ex, axis=0, keepdims=True)
            cs = p[0:1]
            for i in range(1, layer + 1):
                cs = cs + p[i:i + 1]
            fwd = d == 0
            for bi in range(BB):
                chains.append(dict(
                    idx=(bi * HP + hh) * 2 + d, bi=bi, cols=cols, fwd=fwd, lb=cs - p[0:1],
                    q=(qf_ref, qb_ref)[d], f=(ff_ref, fb_ref)[d], v=(vf_ref, vb_ref)[d], o=(of_ref, ob_ref)[d],
                    tri=_tri4(fwd), incl=_order_masks(fwd, CHUNK)[0]))

    def exact_scores(qq, kk, b):
        kk_ref[...] = kk
        b_ref[...] = b

        def col(j, a):
            kj = kk_ref[pl.ds(j, 1), :]
            bj = b_ref[pl.ds(j, 1), :]
            tt = qq * kj * jnp.exp(jnp.minimum(b - bj, 0.0))
            return jnp.where(ci == j, jnp.sum(tt, axis=-1, keepdims=True), a)
        return lax.fori_loop(0, CHUNK, col, jnp.zeros((CHUNK, CHUNK), F32))

    def step_all(step, worst, exact):
        st = []
        for ch in chains:
            c = step if ch["fwd"] else n_chunks - 1 - step
            rows = pl.ds(pl.multiple_of(c * CHUNK, CHUNK), CHUNK)
            bi = ch["bi"]
            sig = _sigmoid(ch["f"][bi, rows, ch["cols"]])
            lb = ch["lb"]
            st.append(dict(
                rows=rows, v=ch["v"][bi, rows, ch["cols"]].astype(BF16), qq=_silu(ch["q"][bi, rows, ch["cols"]]),
                kk=(1.0 - lb) * (1.0 - sig),
                g=jnp.log(jnp.maximum(lb + (1.0 - lb) * sig, LOG_FLOOR))))
        for ch, s in zip(chains, st):
            s["b"] = _dot(ch["tri"], _stack3(s["g"]))
        for ch, s in zip(chains, st):
            fwd, b = ch["fwd"], s["b"]
            bx = b - s["g"]
            s["tot"] = b[CHUNK - 1:CHUNK] if fwd else b[0:1]
            s["refs"] = []
            for i in range(n_sub):
                lo, hi = i * SUB, (i + 1) * SUB
                r_i = bx[lo:lo + 1] if fwd else bx[hi - 1:hi]
                end = b[hi - 1:hi] if fwd else b[lo:lo + 1]
                s["refs"].append(r_i)
                worst = jnp.maximum(worst, r_i - end)
        if exact:
            for s in st:
                s["a"] = exact_scores(s["qq"], s["kk"], s["b"])
        else:
            for s in st:
                s["blocks"] = []
            for i in range(n_sub):
                lo, hi = i * SUB, (i + 1) * SUB
                for s in st:
                    qi = s["qq"][lo:hi] * jnp.exp(s["b"][lo:hi] - s["refs"][i])
                    kt = s["kk"] * jnp.exp(jnp.minimum(s["refs"][i] - s["b"], EXP_CLAMP))
                    s["blocks"].append(_dot_nt(qi.astype(BF16), kt.astype(BF16)))
            for s in st:
                s["a"] = jnp.concatenate(s["blocks"], axis=0)
        for ch, s in zip(chains, st):
            s["st"] = s_ref[ch["idx"]]
            s["o"] = _dot_nt((s["qq"] * jnp.exp(s["b"])).astype(BF16), s["st"].astype(BF16))
        for ch, s in zip(chains, st):
            a = jnp.where(ch["incl"], s["a"], 0.0)
            ch["o"][ch["bi"], s["rows"], ch["cols"]] = s["o"] + _dot(a.astype(BF16), s["v"])
        for ch, s in zip(chains, st):
            kdec = (s["kk"] * jnp.exp(s["tot"] - s["b"])).astype(BF16)
            s_ref[ch["idx"]] = jnp.exp(s["tot"]) * s["st"] + _dot_tn(s["v"], kdec)
        return worst

    def run(exact):
        return lax.fori_loop(0, n_chunks, functools.partial(step_all, exact=exact),
                             jnp.zeros((1, HEAD_DIM), F32))

    s0_ref[...] = s_ref[...]
    worst = run(False)

    @pl.when(jnp.max(worst) > EXP_CLAMP)
    def _():
        s_ref[...] = s0_ref[...]
        run(True)


def _hgrn_scan(u, lb_logits, layer, T=512, HP=4, BB=2):
    B, L, _ = u.shape
    n = L // T
    w = HP * HEAD_DIM
    qf, qb = _bidir_specs(T, n, HEAD_DIM, P_HG_Q, P_HG_Q, HP, BB)
    ff, fb = _bidir_specs(T, n, HEAD_DIM, P_HG_FF, P_HG_FB, HP, BB)
    vf, vb = _bidir_specs(T, n, HEAD_DIM, P_HG_I, P_HG_I, HP, BB)
    of, ob = _bidir_specs(T, n, HEAD_DIM, 0, 0, HP, BB)
    oshape = jax.ShapeDtypeStruct((B, L, GROUP_WIDTH), F32)
    return pl.pallas_call(
        functools.partial(_hgrn_kernel, layer=layer, T=T, HP=HP, BB=BB),
        grid=(B // BB, N_HEADS // HP, n),
        in_specs=[qf, qb, ff, fb, vf, vb,
                  pl.BlockSpec((DEPTH, 2, 1, w), lambda b, hg, t: (0, 0, 0, hg))],
        out_specs=[of, ob],
        out_shape=[oshape, oshape],
        scratch_shapes=[pltpu.VMEM((2 * HP * BB, HEAD_DIM, HEAD_DIM), F32),
                        pltpu.VMEM((2 * HP * BB, HEAD_DIM, HEAD_DIM), F32),
                        pltpu.VMEM((CHUNK, HEAD_DIM), F32),
                        pltpu.VMEM((CHUNK, HEAD_DIM), F32)],
        compiler_params=_cparams("parallel", "parallel", "arbitrary"),
        name="hgrn_scan",
    )(u, u, u, u, u, u, lb_logits.reshape(DEPTH, 2, 1, GROUP_WIDTH))


DN_HALO = 8
DN_GATE_GROUP = 4


def _dn_prep_kernel(*refs, tl):
    w_ref = refs[9]
    outs = refs[10:13]
    t = pl.program_id(1)
    has_prev = t > 0
    has_next = t < pl.num_programs(1) - 1
    ext_rows = tl + 2 * DN_HALO
    for kind in range(3):
        prev_ref, cur_ref, next_ref = refs[3 * kind:3 * kind + 3]
        for h in range(N_HEADS):
            cols = slice(h * HEAD_DIM, (h + 1) * HEAD_DIM)
            wcols = slice(kind * GROUP_WIDTH + h * HEAD_DIM, kind * GROUP_WIDTH + (h + 1) * HEAD_DIM)
            ext = jnp.concatenate([
                jnp.where(has_prev, prev_ref[0, :, cols], 0.0),
                cur_ref[0, :, cols],
                jnp.where(has_next, next_ref[0, :, cols], 0.0)], axis=0)
            acc = None
            for j in range(DN_CONV):
                tap = pltpu.roll(ext, (DN_CONV // 2 - j) % ext_rows, 0)[DN_HALO:DN_HALO + tl]
                term = tap * w_ref[j:j + 1, wcols]
                acc = term if acc is None else acc + term
            y = _silu(acc)
            if kind < 2:
                y = y * lax.rsqrt(jnp.sum(y * y, axis=-1, keepdims=True) + NORM_EPS)
            if kind == 0:
                y = y * ATTN_SCALE
            outs[kind][0, :, cols] = y


def _dn_prep(u, conv_w, tl=512):
    B, L, _ = u.shape
    nh = tl // DN_HALO
    last = L // DN_HALO - 1
    in_specs = []
    for kind in range(3):
        part = P_DN_QKV + kind
        in_specs += [
            pl.BlockSpec((1, DN_HALO, GROUP_WIDTH),
                         lambda b, t, part=part: (b, jnp.maximum(t * nh - 1, 0), part)),
            pl.BlockSpec((1, tl, GROUP_WIDTH), lambda b, t, part=part: (b, t, part)),
            pl.BlockSpec((1, DN_HALO, GROUP_WIDTH),
                         lambda b, t, part=part: (b, jnp.minimum((t + 1) * nh, last), part)),
        ]
    in_specs.append(pl.BlockSpec((DN_CONV, 3 * GROUP_WIDTH), lambda b, t: (0, 0)))
    ospec = pl.BlockSpec((1, tl, GROUP_WIDTH), lambda b, t: (b, t, 0))
    return pl.pallas_call(
        functools.partial(_dn_prep_kernel, tl=tl),
        grid=(B, L // tl),
        in_specs=in_specs,
        out_specs=[ospec, ospec, ospec],
        out_shape=[jax.ShapeDtypeStruct((B, L, GROUP_WIDTH), F32)] * 3,
        compiler_params=_cparams("parallel", "parallel"),
        name="dn_prep",
    )(*([u] * 9), conv_w)


def _softplus(x):
    return jnp.maximum(x, 0.0) + jnp.log1p(jnp.exp(-jnp.abs(x)))


def _dn_gates_kernel(ab_ref, alog_ref, dtb_ref, cols_ref, rows_ref, *, T):
    lane = _iota2((1, HEAD_DIM), 1)
    neg_a = -jnp.exp(alog_ref[...])
    tri_f, tri_b, eye4 = _tri4(True), _tri4(False), _eye4()

    def body(it, carry):
        st = []
        for j in range(DN_GATE_GROUP):
            c = it * DN_GATE_GROUP + j
            rows = pl.ds(pl.multiple_of(c * CHUNK, CHUNK), CHUNK)
            ab = ab_ref[0, rows, :]
            st.append(dict(c=c, rows=rows, ab=ab, g3=_stack3(neg_a * _softplus(ab + dtb_ref[...]))))
        for s in st:
            s["G"] = jnp.where(lane < N_HEADS, _dot(tri_f, s["g3"]), _dot(tri_b, s["g3"]))
        for s in st:
            cols_ref[0, s["rows"], :] = jnp.where(lane < 2 * N_HEADS, s["G"], _sigmoid(s["ab"]))
            rows_ref[0, s["c"]] = _dot_tn(_stack3(s["G"]), eye4)[0:4 * N_HEADS]
        return carry
    lax.fori_loop(0, T // CHUNK // DN_GATE_GROUP, body, 0)


def _dn_gates(uab, a_log, dt_bias, T=1024):
    B, L, _ = uab.shape
    pad = HEAD_DIM - 2 * N_HEADS
    alog = jnp.pad(a_log.astype(F32).reshape(1, 2 * N_HEADS), ((0, 0), (0, pad)))
    dtb = jnp.pad(dt_bias.astype(F32).reshape(1, 2 * N_HEADS), ((0, 0), (0, pad)))
    pspec = pl.BlockSpec((1, HEAD_DIM), lambda b, t: (0, 0))
    return pl.pallas_call(
        functools.partial(_dn_gates_kernel, T=T),
        grid=(B, L // T),
        in_specs=[pl.BlockSpec((1, T, HEAD_DIM), lambda b, t: (b, t, 0)), pspec, pspec],
        out_specs=[pl.BlockSpec((1, T, HEAD_DIM), lambda b, t: (b, t, 0)),
                   pl.BlockSpec((1, T // CHUNK, 4 * N_HEADS, HEAD_DIM), lambda b, t: (b, t, 0, 0))],
        out_shape=[jax.ShapeDtypeStruct((B, L, HEAD_DIM), F32),
                   jax.ShapeDtypeStruct((B, L // CHUNK, 4 * N_HEADS, HEAD_DIM), F32)],
        compiler_params=_cparams("parallel", "parallel"),
        name="dn_gates",
    )(uab, alog, dtb)


def _lhs3(x):
    hi, lo = _split2(x)
    return jnp.concatenate([hi, lo], axis=1)


def _rhs3(x):
    hi, lo = _split2(x)
    return jnp.concatenate([hi, lo, hi, jnp.zeros_like(hi)], axis=0)


def _dn_kernel(qf_ref, qb_ref, kf_ref, kb_ref, vf_ref, vb_ref, gcf_ref, gcb_ref, grf_ref, grb_ref,
               of_ref, ob_ref, s_ref, *, T, HP, BB):
    hg = pl.program_id(1)

    @pl.when(pl.program_id(2) == 0)
    def _():
        s_ref[...] = jnp.zeros_like(s_ref)

    n_chunks = T // CHUNK
    lane = _iota2((1, HEAD_DIM), 1)
    eye = jnp.where(_iota2((CHUNK, HEAD_DIM), 0) == (_iota2((CHUNK, HEAD_DIM), 1) & (CHUNK - 1)), 1.0, 0.0)

    chains = []
    for bi in range(BB):
        for hh in range(HP):
            for d in range(2):
                fwd = d == 0
                incl, strict = _order_masks(fwd, HEAD_DIM)
                chains.append(dict(
                    idx=(bi * HP + hh) * 2 + d, bi=bi, cols=slice(hh * HEAD_DIM, (hh + 1) * HEAD_DIM), fwd=fwd,
                    col=d * N_HEADS + hg * HP + hh, incl=incl, strict=strict,
                    q=(qf_ref, qb_ref)[d], k=(kf_ref, kb_ref)[d], v=(vf_ref, vb_ref)[d],
                    gc=(gcf_ref, gcb_ref)[d], gr=(grf_ref, grb_ref)[d], o=(of_ref, ob_ref)[d]))

    def body(step, carry):
        st = []
        for ch in chains:
            cols, col, bi = ch["cols"], ch["col"], ch["bi"]
            c = step if ch["fwd"] else n_chunks - 1 - step
            rows = pl.ds(pl.multiple_of(c * CHUNK, CHUNK), CHUNK)
            gates = ch["gc"][bi, rows, :]
            k = ch["k"][bi, rows, cols]
            kh = k.astype(BF16)
            st.append(dict(
                rows=rows, q=ch["q"][bi, rows, cols], k=k, v=ch["v"][bi, rows, cols],
                k2=jnp.concatenate([kh, kh], axis=0),
                g_col=jnp.sum(jnp.where(lane == col, gates, 0.0), axis=-1, keepdims=True),
                beta=jnp.sum(jnp.where(lane == col + 2 * N_HEADS, gates, 0.0), axis=-1, keepdims=True),
                g_row=ch["gr"][bi, c, pl.ds(col, 1), :]))
        for ch, s in zip(chains, st):
            s["gamma"] = jnp.where(ch["incl"], jnp.exp(jnp.minimum(s["g_col"] - s["g_row"], 0.0)), 0.0)
            s["kb"] = s["k"] * s["beta"]
        for ch, s in zip(chains, st):
            n_mat = jnp.where(ch["strict"], _dot_nt(s["kb"].astype(BF16), s["k2"]) * s["gamma"], 0.0)
            s["x"] = eye - n_mat
            s["nl"], s["nr"] = _lhs3(n_mat), _rhs3(n_mat)

        order = 1
        while True:
            for s in st:
                s["nk"] = _dot(s["nl"], s["nr"])
            for s in st:
                s["nr"] = _rhs3(s["nk"])
                s["x"] = s["x"] + _dot(_lhs3(s["x"]), s["nr"])
            order *= 2
            if 2 * order >= CHUNK:
                break
            for s in st:
                s["nl"] = _lhs3(s["nk"])

        for s in st:
            s["eg"] = jnp.exp(s["g_col"])
            rhs = jnp.concatenate([s["kb"] * s["eg"], s["v"] * s["beta"]], axis=1)
            s["wu"] = _dot(_lhs3(s["x"]), _rhs3(rhs))
        for ch, s in zip(chains, st):
            s["a_qk"] = (_dot_nt(s["q"].astype(BF16), s["k2"]) * s["gamma"])[:, 0:CHUNK].astype(BF16)
            s["s"] = s_ref[ch["idx"]]
            s["sb"] = s["s"].astype(BF16)
        for s in st:
            s["v_new"] = (s["wu"][:, HEAD_DIM:] - _dot(s["wu"][:, :HEAD_DIM].astype(BF16), s["sb"])).astype(BF16)
        for ch, s in zip(chains, st):
            ch["o"][ch["bi"], s["rows"], ch["cols"]] = (_dot((s["q"] * s["eg"]).astype(BF16), s["sb"])
                                                 + _dot(s["a_qk"], s["v_new"]))
        for ch, s in zip(chains, st):
            g_last = s["g_col"][CHUNK - 1:CHUNK] if ch["fwd"] else s["g_col"][0:1]
            k_dec = (s["k"] * jnp.exp(g_last - s["g_col"])).astype(BF16)
            s_ref[ch["idx"]] = jnp.exp(g_last) * s["s"] + _dot_tn(k_dec, s["v_new"])
        return carry
    lax.fori_loop(0, n_chunks, body, 0)


def _dn_scan(qh, kh, vh, gate_cols, gate_rows, T=512, HP=4, BB=2):
    B, L, _ = qh.shape
    n = L // T
    hf, hb = _bidir_specs(T, n, HEAD_DIM, 0, 0, HP, BB)
    gcf = pl.BlockSpec((BB, T, HEAD_DIM), lambda b, hg, t: (b, t, 0))
    gcb = pl.BlockSpec((BB, T, HEAD_DIM), lambda b, hg, t: (b, n - 1 - t, 0))
    rshape = (BB, T // CHUNK, 4 * N_HEADS, HEAD_DIM)
    grf = pl.BlockSpec(rshape, lambda b, hg, t: (b, t, 0, 0))
    grb = pl.BlockSpec(rshape, lambda b, hg, t: (b, n - 1 - t, 0, 0))
    oshape = jax.ShapeDtypeStruct((B, L, GROUP_WIDTH), F32)
    nc = 2 * HP * BB
    return pl.pallas_call(
        functools.partial(_dn_kernel, T=T, HP=HP, BB=BB),
        grid=(B // BB, N_HEADS // HP, n),
        in_specs=[hf, hb, hf, hb, hf, hb, gcf, gcb, grf, grb],
        out_specs=[hf, hb],
        out_shape=[oshape, oshape],
        scratch_shapes=[pltpu.VMEM((nc, HEAD_DIM, HEAD_DIM), F32)],
        compiler_params=_cparams("parallel", "parallel", "arbitrary"),
        name="dn_scan",
    )(qh, qh, kh, kh, vh, vh, gate_cols, gate_cols, gate_rows, gate_rows)


def _encoder_layer(x, mod, layer, p, rope_tabs):
    u, uab = _in_proj(x, p["norm1_g"][layer], mod, p["w_in_main"], p["w_in_ab"][layer], layer)

    swa_q, swa_k = _qk_prep(u, P_SWA_Q, P_SWA_K, None, p["swa_q_norm"][layer], p["swa_k_norm"][layer],
                            rope_tabs, F32)
    y_swa = _swa_attention(swa_q, swa_k, u)

    o_hg = _hgrn_scan(u, p["hgrn_lb_logits"], layer)

    na_q, na_k, na_v = _qk_prep(u, P_NA_Q, P_NA_K, P_NA_V, p["na_q_norm"][layer], p["na_k_norm"][layer],
                                None, BF16)
    y_na = _na_attention(na_q, na_k, na_v, p["na_bias"][layer])

    dq, dk, dv = _dn_prep(u, p["dn_conv_w"][layer])
    gate_cols, gate_rows = _dn_gates(uab, p["dn_a_log"][layer], p["dn_dt_bias"][layer])
    o_dn = _dn_scan(dq, dk, dv, gate_cols, gate_rows)

    x = _out_proj(x, mod, u, y_swa, y_na, o_hg, p["hgrn_norm_g"][layer], o_dn, p["dn_norm_g"][layer],
                  p["w_out"], layer)
    return _mlp(x, p["norm2_g"][layer], mod, p["w_mlp_in"], p["w_mlp_out"], layer)


def _run_trunk(x, mod_all, p):
    B, L, _ = x.shape
    rope_tabs = _rope_tables(L)
    for layer in range(DEPTH):
        x = _encoder_layer(x, mod_all[layer].reshape(B, 6, D_MODEL), layer, p, rope_tabs)
    return x


def kernel(x_prompt, x_sample, c_prompt, c_sample, norm1_g, norm2_g, ada_w, ada_b, w_in, w_out,
           swa_q_norm, swa_k_norm, hgrn_lb_logits, hgrn_norm_g, na_q_norm, na_k_norm, na_rpb,
           dn_conv_w, dn_a_log, dn_dt_bias, dn_norm_g, w_mlp_in, w_mlp_out):
    bp, bs = c_prompt.shape[0], c_sample.shape[0]
    pad_rows = -(bp + bs) % 8
    c_all = jnp.concatenate([c_prompt, c_sample, jnp.zeros((pad_rows, D_MODEL), F32)], axis=0)
    mod_all = _modulation(c_all, ada_w, ada_b)

    p = dict(
        norm1_g=norm1_g, norm2_g=norm2_g,
        w_in_main=w_in.astype(BF16),
        w_in_ab=jnp.pad(w_in[:, :, IN_MAIN:], ((0, 0), (0, 0), (0, HEAD_DIM - 4 * N_HEADS))).astype(BF16),
        w_out=w_out.astype(BF16),
        swa_q_norm=swa_q_norm, swa_k_norm=swa_k_norm,
        hgrn_lb_logits=hgrn_lb_logits.astype(F32), hgrn_norm_g=hgrn_norm_g,
        na_q_norm=na_q_norm, na_k_norm=na_k_norm,
        na_bias=jnp.stack([_na_bias_table(na_rpb[l]) for l in range(DEPTH)]),
        dn_conv_w=dn_conv_w, dn_a_log=dn_a_log, dn_dt_bias=dn_dt_bias, dn_norm_g=dn_norm_g,
        w_mlp_in=w_mlp_in.astype(BF16), w_mlp_out=w_mlp_out.astype(BF16),
    )
    y_prompt = _run_trunk(x_prompt, mod_all[:, :bp], p)
    y_sample = _run_trunk(x_sample, mod_all[:, bp:bp + bs], p)
    return (y_prompt, y_sample)
```

```python
import functools
import math

import jax
import jax.numpy as jnp
from jax import lax
from jax.experimental import pallas as pl
from jax.experimental.pallas import tpu as pltpu

F32 = jnp.float32
BF16 = jnp.bfloat16

D_MODEL = 2048
DEPTH = 4
HEAD_DIM = 128
N_HEADS = 4
GROUP_WIDTH = N_HEADS * HEAD_DIM
IN_MAIN = 15 * GROUP_WIDTH
D_FF = 4 * D_MODEL
NORM_EPS = 1e-6
ROPE_THETA = 500000.0
ROPE_DIM = HEAD_DIM // 4
DILATED_CONFIGS = ((128, 1), (512, 4), (2048, 16))
GRID_W = 64
NA_KH = 8
NA_KW = 16
CHUNK = 64
SUB = 16
DN_CONV = 5
MASK_VALUE = -1e30
LOG_FLOOR = 1e-30
ATTN_SCALE = HEAD_DIM ** -0.5
EXP_CLAMP = 60.0
HIGHEST = lax.Precision.HIGHEST

P_SWA_Q, P_SWA_K, P_SWA_V = 0, 1, 2
P_HG_Q, P_HG_FF, P_HG_FB, P_HG_I, P_HG_G = 3, 4, 5, 6, 7
P_NA_Q, P_NA_K, P_NA_V = 8, 9, 10
P_DN_QKV, P_DN_Z = 11, 14

VMEM_LIMIT = 56 * 1024 * 1024


def _cparams(*sem):
    return pltpu.CompilerParams(dimension_semantics=sem, vmem_limit_bytes=VMEM_LIMIT)


def _dot(a, b):
    return jnp.dot(a, b, preferred_element_type=F32)


def _dot_nt(a, b):
    return lax.dot_general(a, b, (((1,), (1,)), ((), ())), preferred_element_type=F32)


def _dot_tn(a, b):
    return lax.dot_general(a, b, (((0,), (0,)), ((), ())), preferred_element_type=F32)


def _sigmoid(x):
    return jax.nn.sigmoid(x)


def _silu(x):
    return x * _sigmoid(x)


def _rms(x):
    return x * lax.rsqrt(jnp.mean(x * x, axis=-1, keepdims=True) + NORM_EPS)


def _split2(x):
    hi = x.astype(BF16)
    return hi, (x - hi.astype(F32)).astype(BF16)


def _split3(x):
    hi = x.astype(BF16)
    r1 = x - hi.astype(F32)
    mid = r1.astype(BF16)
    return hi, mid, (r1 - mid.astype(F32)).astype(BF16)


def _stack3(x):
    hi, mid, lo = _split3(x)
    return jnp.concatenate([hi, mid, lo, jnp.zeros_like(hi)], axis=0)


def _iota2(shape, axis):
    return lax.broadcasted_iota(jnp.int32, shape, axis)


def _tri4(fwd):
    ri, ci = _iota2((CHUNK, 4 * CHUNK), 0), _iota2((CHUNK, 4 * CHUNK), 1)
    cm = ci & (CHUNK - 1)
    order = (cm <= ri) if fwd else (cm >= ri)
    return jnp.where(order & (ci < 3 * CHUNK), 1.0, 0.0).astype(BF16)


def _eye4():
    ri, ci = _iota2((4 * CHUNK, 2 * CHUNK), 0), _iota2((4 * CHUNK, 2 * CHUNK), 1)
    same = (ri & (CHUNK - 1)) == (ci & (CHUNK - 1))
    return jnp.where(same & (ri < 3 * CHUNK), 1.0, 0.0).astype(BF16)


def _order_masks(fwd, width):
    ri, ci = _iota2((CHUNK, width), 0), _iota2((CHUNK, width), 1) & (CHUNK - 1)
    if fwd:
        return ci <= ri, ci < ri
    return ci >= ri, ci > ri


def _mod_kernel(c_ref, w_ref, b_ref, o_ref):
    a = _silu(c_ref[...]).astype(BF16)
    o_ref[0] = _dot(a, w_ref[0].astype(BF16)) + b_ref[0]


def _modulation(c_all, ada_w, ada_b):
    rows, d = c_all.shape
    n = ada_w.shape[-1]
    tn = 1024
    return pl.pallas_call(
        _mod_kernel,
        grid=(DEPTH, n // tn),
        in_specs=[
            pl.BlockSpec((rows, d), lambda l, j: (0, 0)),
            pl.BlockSpec((1, d, tn), lambda l, j: (l, 0, j)),
            pl.BlockSpec((1, 1, tn), lambda l, j: (l, 0, j)),
        ],
        out_specs=pl.BlockSpec((1, rows, tn), lambda l, j: (l, 0, j)),
        out_shape=jax.ShapeDtypeStruct((DEPTH, rows, n), F32),
        compiler_params=_cparams("parallel", "parallel"),
        name="adaln_mod",
    )(c_all, ada_w, ada_b.reshape(DEPTH, 1, n))


NORM_ROWS = 128


def _norm_mod_store(h_ref, x_ref, g, scale, shift, tm):
    gain = g * (1.0 + scale)

    def body(c, carry):
        rows = pl.ds(pl.multiple_of(c * NORM_ROWS, NORM_ROWS), NORM_ROWS)
        x = x_ref[0, rows, :]
        inv = lax.rsqrt(jnp.mean(x * x, axis=-1, keepdims=True) + NORM_EPS)
        h_ref[rows, :] = (x_ref[0, rows, :] * inv * gain + shift).astype(BF16)
        return carry
    lax.fori_loop(0, tm // NORM_ROWS, body, 0)


def _in_proj_kernel(x_ref, g_ref, mod_ref, w_ref, wab_ref, u_ref, uab_ref, h_ref, *, tm):
    @pl.when(pl.program_id(2) == 0)
    def _():
        _norm_mod_store(h_ref, x_ref, g_ref[...], mod_ref[0, 1:2, :], mod_ref[0, 0:1, :], tm)
        uab_ref[0] = _dot(h_ref[...], wab_ref[...])
    u_ref[0] = _dot(h_ref[...], w_ref[...])


def _in_proj(x, g, mod, w_main, w_ab, layer, tm=512, tn=2560):
    B, L, D = x.shape
    N = IN_MAIN
    return pl.pallas_call(
        functools.partial(_in_proj_kernel, tm=tm),
        grid=(B, L // tm, N // tn),
        in_specs=[
            pl.BlockSpec((1, tm, D), lambda b, i, j: (b, i, 0)),
            pl.BlockSpec((1, D), lambda b, i, j: (0, 0)),
            pl.BlockSpec((1, 6, D), lambda b, i, j: (b, 0, 0)),
            pl.BlockSpec((None, D, tn), lambda b, i, j: (layer, 0, j)),
            pl.BlockSpec((D, HEAD_DIM), lambda b, i, j: (0, 0)),
        ],
        out_specs=[
            pl.BlockSpec((1, tm, tn), lambda b, i, j: (b, i, j)),
            pl.BlockSpec((1, tm, HEAD_DIM), lambda b, i, j: (b, i, 0)),
        ],
        out_shape=[jax.ShapeDtypeStruct((B, L, N), F32),
                   jax.ShapeDtypeStruct((B, L, HEAD_DIM), F32)],
        scratch_shapes=[pltpu.VMEM((tm, D), BF16)],
        compiler_params=_cparams("parallel", "parallel", "arbitrary"),
        name="in_proj",
    )(x, g.reshape(1, D), mod, w_main, w_ab)


def _gated_heads(of_ref, ob_ref, z_ref, g_ref):
    parts = []
    for h in range(N_HEADS):
        cols = slice(h * HEAD_DIM, (h + 1) * HEAD_DIM)
        o = of_ref[0, :, cols] + ob_ref[0, :, cols]
        parts.append((_rms(o) * g_ref[...] * _silu(z_ref[0, :, cols])).astype(BF16))
    return jnp.concatenate(parts, axis=1)


def _out_proj_kernel(x_ref, mod_ref, swa_ref, na_ref, hgf_ref, hgb_ref, hgz_ref, hgg_ref,
                     dnf_ref, dnb_ref, dnz_ref, dng_ref, w_ref, o_ref):
    ys = (swa_ref[0], _gated_heads(hgf_ref, hgb_ref, hgz_ref, hgg_ref),
          na_ref[0], _gated_heads(dnf_ref, dnb_ref, dnz_ref, dng_ref))
    acc = None
    for m, y in enumerate(ys):
        part = _dot(y, w_ref[m * GROUP_WIDTH:(m + 1) * GROUP_WIDTH, :])
        acc = part if acc is None else acc + part
    o_ref[0] = x_ref[0] + mod_ref[0, 2:3, :] * acc


def _out_proj(x, mod, u, y_swa, y_na, o_hg, g_hg, o_dn, g_dn, w_out, layer, tm=512):
    B, L, D = x.shape
    yspec = pl.BlockSpec((1, tm, GROUP_WIDTH), lambda b, i: (b, i, 0))
    gspec = pl.BlockSpec((1, HEAD_DIM), lambda b, i: (0, 0))
    zspec = lambda part: pl.BlockSpec((1, tm, GROUP_WIDTH), lambda b, i: (b, i, part))
    return pl.pallas_call(
        _out_proj_kernel,
        grid=(B, L // tm),
        in_specs=[
            pl.BlockSpec((1, tm, D), lambda b, i: (b, i, 0)),
            pl.BlockSpec((1, 6, D), lambda b, i: (b, 0, 0)),
            yspec, yspec,
            yspec, yspec, zspec(P_HG_G), gspec,
            yspec, yspec, zspec(P_DN_Z), gspec,
            pl.BlockSpec((None, D, D), lambda b, i: (layer, 0, 0)),
        ],
        out_specs=pl.BlockSpec((1, tm, D), lambda b, i: (b, i, 0)),
        out_shape=jax.ShapeDtypeStruct((B, L, D), F32),
        compiler_params=_cparams("parallel", "parallel"),
        name="out_proj",
    )(x, mod, y_swa, y_na, *o_hg, u, g_hg.reshape(1, HEAD_DIM), *o_dn, u, g_dn.reshape(1, HEAD_DIM), w_out)


def _mlp_kernel(x_ref, g_ref, mod_ref, w1_ref, w2_ref, o_ref, h_ref, *, tm):
    j = pl.program_id(2)

    @pl.when(j == 0)
    def _():
        _norm_mod_store(h_ref, x_ref, g_ref[...], mod_ref[0, 4:5, :], mod_ref[0, 3:4, :], tm)
        o_ref[...] = jnp.zeros_like(o_ref)

    half = w1_ref.shape[1] // 2
    hids = [jnp.square(jnp.maximum(_dot(h_ref[...], w1_ref[:, c * half:(c + 1) * half]), 0.0)).astype(BF16)
            for c in range(2)]
    o_ref[0] += _dot(hids[0], w2_ref[0:half, :]) + _dot(hids[1], w2_ref[half:, :])

    @pl.when(j == pl.num_programs(2) - 1)
    def _():
        o_ref[0] = x_ref[0] + mod_ref[0, 5:6, :] * o_ref[0]


def _mlp(x, g, mod, w1, w2, layer, tm=512, tf=2048):
    B, L, D = x.shape
    F = w1.shape[-1]
    return pl.pallas_call(
        functools.partial(_mlp_kernel, tm=tm),
        grid=(B, L // tm, F // tf),
        in_specs=[
            pl.BlockSpec((1, tm, D), lambda b, i, j: (b, i, 0)),
            pl.BlockSpec((1, D), lambda b, i, j: (0, 0)),
            pl.BlockSpec((1, 6, D), lambda b, i, j: (b, 0, 0)),
            pl.BlockSpec((None, D, tf), lambda b, i, j: (layer, 0, j)),
            pl.BlockSpec((None, tf, D), lambda b, i, j: (layer, j, 0)),
        ],
        out_specs=pl.BlockSpec((1, tm, D), lambda b, i, j: (b, i, 0)),
        out_shape=jax.ShapeDtypeStruct((B, L, D), F32),
        scratch_shapes=[pltpu.VMEM((tm, D), BF16)],
        compiler_params=_cparams("parallel", "parallel", "arbitrary"),
        name="mlp",
    )(x, g.reshape(1, D), mod, w1, w2)


def _qk_prep_kernel(*refs, rope):
    if rope:
        q_ref, k_ref, gq_ref, gk_ref, c_ref, s1_ref, s2_ref, qo_ref, ko_ref = refs
    else:
        q_ref, k_ref, v_ref, gq_ref, gk_ref, qo_ref, ko_ref, vo_ref = refs
        vo_ref[...] = v_ref[...].astype(vo_ref.dtype)
    ones = jnp.ones((HEAD_DIM, HEAD_DIM), BF16)
    for src, g_ref, dst in ((q_ref, gq_ref, qo_ref), (k_ref, gk_ref, ko_ref)):
        for h in range(N_HEADS):
            cols = slice(h * HEAD_DIM, (h + 1) * HEAD_DIM)
            x = src[0, :, cols]
            hi, lo = _split2(x * x)
            ss = _dot(hi, ones) + _dot(lo, ones)
            y = x * lax.rsqrt(ss * (1.0 / HEAD_DIM) + NORM_EPS) * g_ref[...]
            if rope:
                y = (y * c_ref[...] + pltpu.roll(y, ROPE_DIM // 2, 1) * s1_ref[...]
                     + pltpu.roll(y, HEAD_DIM - ROPE_DIM // 2, 1) * s2_ref[...])
            dst[0, :, cols] = y.astype(dst.dtype)


def _rope_tables(L):
    half = ROPE_DIM // 2
    inv_freq = jnp.power(ROPE_THETA, -jnp.arange(half, dtype=F32) / half)
    ang = jnp.arange(L).astype(F32)[:, None] * inv_freq[None, :]
    cos, sin = jnp.cos(ang), jnp.sin(ang)
    zeros = jnp.zeros((L, HEAD_DIM - ROPE_DIM), F32)
    zh = jnp.zeros((L, half), F32)
    c = jnp.concatenate([cos, cos, jnp.ones((L, HEAD_DIM - ROPE_DIM), F32)], axis=1)
    s1 = jnp.concatenate([zh, sin, zeros], axis=1)
    s2 = jnp.concatenate([-sin, zh, zeros], axis=1)
    return c, s1, s2


def _qk_prep(u, part_q, part_k, part_v, gq, gk, rope_tabs, out_dtype, tl=512):
    B, L, _ = u.shape
    rope = rope_tabs is not None
    uspec = lambda part: pl.BlockSpec((1, tl, GROUP_WIDTH), lambda b, t, part=part: (b, t, part))
    gspec = pl.BlockSpec((1, HEAD_DIM), lambda b, t: (0, 0))
    tspec = pl.BlockSpec((tl, HEAD_DIM), lambda b, t: (t, 0))
    ospec = pl.BlockSpec((1, tl, GROUP_WIDTH), lambda b, t: (b, t, 0))
    gains = [gq.reshape(1, HEAD_DIM), gk.reshape(1, HEAD_DIM)]
    if rope:
        in_specs = [uspec(part_q), uspec(part_k), gspec, gspec, tspec, tspec, tspec]
        args = [u, u] + gains + list(rope_tabs)
    else:
        in_specs = [uspec(part_q), uspec(part_k), uspec(part_v), gspec, gspec]
        args = [u, u, u] + gains
    n_out = 2 if rope else 3
    return pl.pallas_call(
        functools.partial(_qk_prep_kernel, rope=rope),
        grid=(B, L // tl),
        in_specs=in_specs,
        out_specs=[ospec] * n_out,
        out_shape=[jax.ShapeDtypeStruct((B, L, GROUP_WIDTH), out_dtype)] * n_out,
        compiler_params=_cparams("parallel", "parallel"),
        name="qk_prep_rope" if rope else "qkv_prep",
    )(*args)


SWA_Q = 128
SWA_K = 2 * SWA_Q
SWA_RADIUS = 64
SWA_ROWS = 256
SWA_GROUP = 8


def _swa_kernel(q_ref, k_ref, v_ref, o_ref, ob_ref, ls_ref, mask_ref, *, L, T):
    t0 = pl.program_id(2) * T
    diff = _iota2((SWA_Q, SWA_K), 1) - _iota2((SWA_Q, SWA_K), 0)
    for i in range(3):
        mask_ref[i] = jnp.where(jnp.abs(diff - i * SWA_RADIUS) <= SWA_RADIUS, 0.0, MASK_VALUE)

    for bi, (window, dil) in enumerate(DILATED_CONFIGS):
        assert window // (2 * dil) == SWA_RADIUS
        span = SWA_Q * dil
        n_sub = T // span

        def body(it, carry, dil=dil, bi=bi, span=span, n_sub=n_sub):
            st = []
            for j in range(SWA_GROUP):
                idx = it * SWA_GROUP + j
                r, loc = idx // n_sub, (idx % n_sub) * span
                base = jnp.clip(t0 + loc - SWA_RADIUS * dil, 0, L - SWA_K * dil)
                if dil == 1:
                    qs = pl.ds(pl.multiple_of(loc, SWA_Q), SWA_Q)
                    ks = pl.ds(pl.multiple_of(base, SWA_RADIUS), SWA_K)
                else:
                    qs = pl.ds(loc + r, SWA_Q, stride=dil)
                    ks = pl.ds(base + r, SWA_K, stride=dil)
                st.append(dict(qs=qs, ks=ks, mask=(t0 + loc - base) // (dil * SWA_RADIUS)))
            for s in st:
                sc = _dot_nt(q_ref[0, s["qs"], :].astype(BF16), k_ref[0, s["ks"], :].astype(BF16))
                s["s"] = sc * ATTN_SCALE + mask_ref[s["mask"]]
            for s in st:
                s["m"] = jnp.max(s["s"], axis=-1, keepdims=True)
                s["e"] = jnp.exp(s["s"] - s["m"])
                s["den"] = jnp.sum(s["e"], axis=-1, keepdims=True)
            for s in st:
                pv = _dot(s["e"].astype(BF16), v_ref[0, s["ks"], :].astype(BF16))
                ob_ref[bi, s["qs"], :] = pv / s["den"]
                ls_ref[bi, s["qs"], :] = jnp.broadcast_to(s["m"] + jnp.log(s["den"]), (SWA_Q, HEAD_DIM))
            return carry
        lax.fori_loop(0, T // SWA_Q // SWA_GROUP, body, 0)

    def combine(c, carry):
        rows = pl.ds(pl.multiple_of(c * SWA_ROWS, SWA_ROWS), SWA_ROWS)
        l0, l1, l2 = ls_ref[0, rows, :], ls_ref[1, rows, :], ls_ref[2, rows, :]
        mx = jnp.maximum(l0, jnp.maximum(l1, l2))
        w0, w1, w2 = jnp.exp(l0 - mx), jnp.exp(l1 - mx), jnp.exp(l2 - mx)
        num = w0 * ob_ref[0, rows, :] + w1 * ob_ref[1, rows, :] + w2 * ob_ref[2, rows, :]
        o_ref[0, rows, :] = (num / (w0 + w1 + w2)).astype(o_ref.dtype)
        return carry
    lax.fori_loop(0, T // SWA_ROWS, combine, 0)


def _swa_attention(qn, kn, u, T=2048):
    B, L, _ = qn.shape
    max_dil = max(d for _, d in DILATED_CONFIGS)
    assert L % T == 0 and T % (SWA_Q * max_dil) == 0 and L >= SWA_K * max_dil
    nb = len(DILATED_CONFIGS)
    return pl.pallas_call(
        functools.partial(_swa_kernel, L=L, T=T),
        grid=(B, N_HEADS, L // T),
        in_specs=[
            pl.BlockSpec((1, T, HEAD_DIM), lambda b, h, t: (b, t, h)),
            pl.BlockSpec((1, L, HEAD_DIM), lambda b, h, t: (b, 0, h)),
            pl.BlockSpec((1, L, HEAD_DIM), lambda b, h, t: (b, 0, P_SWA_V * N_HEADS + h)),
        ],
        out_specs=pl.BlockSpec((1, T, HEAD_DIM), lambda b, h, t: (b, t, h)),
        out_shape=jax.ShapeDtypeStruct((B, L, GROUP_WIDTH), BF16),
        scratch_shapes=[pltpu.VMEM((nb, T, HEAD_DIM), F32), pltpu.VMEM((nb, T, HEAD_DIM), F32),
                        pltpu.VMEM((3, SWA_Q, SWA_K), F32)],
        compiler_params=_cparams("parallel", "parallel", "arbitrary"),
        name="swa_attention",
    )(qn, kn, u)


NA_KEYS = NA_KH * GRID_W
NA_GROUP = 8


def _na_bias_table(rpb):
    cq = jnp.arange(GRID_W)
    ck = jnp.arange(GRID_W)
    c0 = jnp.clip(cq - NA_KW // 2, 0, GRID_W - NA_KW)
    col_ok = (ck[None, :] >= c0[:, None]) & (ck[None, :] < c0[:, None] + NA_KW)
    col_off = jnp.clip(ck[None, :] - cq[:, None], -(NA_KW - 1), NA_KW - 1) + (NA_KW - 1)
    onehot = (col_off[None] == jnp.arange(2 * NA_KW - 1)[:, None, None]).astype(F32)
    cols = jnp.einsum('hrc,cqw->hrqw', rpb.astype(F32), onehot, precision=lax.Precision.HIGHEST)
    cols = jnp.where(col_ok[None, None], cols, MASK_VALUE)
    bias = jnp.stack([cols[:, s:s + NA_KH] for s in range(NA_KH)], axis=1)
    bias = bias.transpose(0, 1, 3, 2, 4)
    return bias.reshape(rpb.shape[0], NA_KH, GRID_W, NA_KEYS)


def _na_kernel(q_ref, k_ref, v_ref, bias_ref, o_ref, *, rows, R):
    t = pl.program_id(2)

    def body(it, carry):
        st = []
        for j in range(NA_GROUP):
            i = it * NA_GROUP + j
            r = t * R + i
            r0 = jnp.clip(r - NA_KH // 2, 0, rows - NA_KH)
            st.append(dict(shift=r0 - r + (NA_KH - 1),
                           qs=pl.ds(pl.multiple_of(i * GRID_W, GRID_W), GRID_W),
                           ks=pl.ds(pl.multiple_of(r0 * GRID_W, GRID_W), NA_KEYS)))
        for s in st:
            s["s"] = _dot_nt(q_ref[0, s["qs"], :], k_ref[0, s["ks"], :]) * ATTN_SCALE + bias_ref[0, s["shift"]]
        for s in st:
            s["m"] = jnp.max(s["s"], axis=-1, keepdims=True)
            s["e"] = jnp.exp(s["s"] - s["m"])
            s["den"] = jnp.sum(s["e"], axis=-1, keepdims=True)
        for s in st:
            o_ref[0, s["qs"], :] = (_dot(s["e"].astype(BF16), v_ref[0, s["ks"], :]) / s["den"]).astype(o_ref.dtype)
        return carry
    lax.fori_loop(0, R // NA_GROUP, body, 0)


def _na_attention(qn, kn, vn, bias, R=16):
    B, L, _ = qn.shape
    rows = L // GRID_W
    assert rows >= NA_KH and rows % R == 0
    T = R * GRID_W
    return pl.pallas_call(
        functools.partial(_na_kernel, rows=rows, R=R),
        grid=(B, N_HEADS, rows // R),
        in_specs=[
            pl.BlockSpec((1, T, HEAD_DIM), lambda b, h, t: (b, t, h)),
            pl.BlockSpec((1, L, HEAD_DIM), lambda b, h, t: (b, 0, h)),
            pl.BlockSpec((1, L, HEAD_DIM), lambda b, h, t: (b, 0, h)),
            pl.BlockSpec((1, NA_KH, GRID_W, NA_KEYS), lambda b, h, t: (h, 0, 0, 0)),
        ],
        out_specs=pl.BlockSpec((1, T, HEAD_DIM), lambda b, h, t: (b, t, h)),
        out_shape=jax.ShapeDtypeStruct((B, L, GROUP_WIDTH), BF16),
        compiler_params=_cparams("parallel", "parallel", "arbitrary"),
        name="na_attention",
    )(qn, kn, vn, bias)


def _bidir_specs(T, n, width, part_f, part_b, heads_per_step, batch_per_step=1):
    w = heads_per_step * width
    bb = batch_per_step
    return (pl.BlockSpec((bb, T, w), lambda b, hg, t: (b, t, part_f * (GROUP_WIDTH // w) + hg)),
            pl.BlockSpec((bb, T, w), lambda b, hg, t: (b, n - 1 - t, part_b * (GROUP_WIDTH // w) + hg)))


def _hgrn_kernel(qf_ref, qb_ref, ff_ref, fb_ref, vf_ref, vb_ref, lg_ref, of_ref, ob_ref,
                 s_ref, s0_ref, kk_ref, b_ref, *, layer, T, HP, BB):
    @pl.when(pl.program_id(2) == 0)
    def _():
        s_ref[...] = jnp.zeros_like(s_ref)

    n_chunks = T // CHUNK
    n_sub = CHUNK // SUB
    ci = _iota2((CHUNK, CHUNK), 1)

    chains = []
    for hh in range(HP):
        cols = slice(hh * HEAD_DIM, (hh + 1) * HEAD_DIM)
        for d in range(2):
            lg = lg_ref[:, d, 0, cols]
            ex = jnp.exp(lg - jnp.max(lg, axis=0, keepdims=True))
            p = ex / jnp.sum(ex, axis=0, keepdims=True)
            cs = p[0:1]
            for i in range(1, layer + 1):
                cs = cs + p[i:i + 1]
            fwd = d == 0
            for bi in range(BB):
                chains.append(dict(
                    idx=(bi * HP + hh) * 2 + d, bi=bi, cols=cols, fwd=fwd, lb=cs - p[0:1],
                    q=(qf_ref, qb_ref)[d], f=(ff_ref, fb_ref)[d], v=(vf_ref, vb_ref)[d], o=(of_ref, ob_ref)[d],
                    tri=_tri4(fwd), incl=_order_masks(fwd, CHUNK)[0]))

    def exact_scores(qq, kk, b):
        kk_ref[...] = kk
        b_ref[...] = b

        def col(j, a):
            kj = kk_ref[pl.ds(j, 1), :]
            bj = b_ref[pl.ds(j, 1), :]
            tt = qq * kj * jnp.exp(jnp.minimum(b - bj, 0.0))
            return jnp.where(ci == j, jnp.sum(tt, axis=-1, keepdims=True), a)
        return lax.fori_loop(0, CHUNK, col, jnp.zeros((CHUNK, CHUNK), F32))

    def step_all(step, worst, exact):
        st = []
        for ch in chains:
            c = step if ch["fwd"] else n_chunks - 1 - step
            rows = pl.ds(pl.multiple_of(c * CHUNK, CHUNK), CHUNK)
            bi = ch["bi"]
            sig = _sigmoid(ch["f"][bi, rows, ch["cols"]])
            lb = ch["lb"]
            st.append(dict(
                rows=rows, v=ch["v"][bi, rows, ch["cols"]].astype(BF16), qq=_silu(ch["q"][bi, rows, ch["cols"]]),
                kk=(1.0 - lb) * (1.0 - sig),
                g=jnp.log(jnp.maximum(lb + (1.0 - lb) * sig, LOG_FLOOR))))
        for ch, s in zip(chains, st):
            s["b"] = _dot(ch["tri"], _stack3(s["g"]))
        for ch, s in zip(chains, st):
            fwd, b = ch["fwd"], s["b"]
            bx = b - s["g"]
            s["tot"] = b[CHUNK - 1:CHUNK] if fwd else b[0:1]
            s["refs"] = []
            for i in range(n_sub):
                lo, hi = i * SUB, (i + 1) * SUB
                r_i = bx[lo:lo + 1] if fwd else bx[hi - 1:hi]
                end = b[hi - 1:hi] if fwd else b[lo:lo + 1]
                s["refs"].append(r_i)
                worst = jnp.maximum(worst, r_i - end)
        if exact:
            for s in st:
                s["a"] = exact_scores(s["qq"], s["kk"], s["b"])
        else:
            for s in st:
                s["blocks"] = []
            for i in range(n_sub):
                lo, hi = i * SUB, (i + 1) * SUB
                for s in st:
                    qi = s["qq"][lo:hi] * jnp.exp(s["b"][lo:hi] - s["refs"][i])
                    kt = s["kk"] * jnp.exp(jnp.minimum(s["refs"][i] - s["b"], EXP_CLAMP))
                    s["blocks"].append(_dot_nt(qi.astype(BF16), kt.astype(BF16)))
            for s in st:
                s["a"] = jnp.concatenate(s["blocks"], axis=0)
        for ch, s in zip(chains, st):
            s["st"] = s_ref[ch["idx"]]
            s["o"] = _dot_nt((s["qq"] * jnp.exp(s["b"])).astype(BF16), s["st"].astype(BF16))
        for ch, s in zip(chains, st):
            a = jnp.where(ch["incl"], s["a"], 0.0)
            ch["o"][ch["bi"], s["rows"], ch["cols"]] = s["o"] + _dot(a.astype(BF16), s["v"])
        for ch, s in zip(chains, st):
            kdec = (s["kk"] * jnp.exp(s["tot"] - s["b"])).astype(BF16)
            s_ref[ch["idx"]] = jnp.exp(s["tot"]) * s["st"] + _dot_tn(s["v"], kdec)
        return worst

    def run(exact):
        return lax.fori_loop(0, n_chunks, functools.partial(step_all, exact=exact),
                             jnp.zeros((1, HEAD_DIM), F32))

    s0_ref[...] = s_ref[...]
    worst = run(False)

    @pl.when(jnp.max(worst) > EXP_CLAMP)
    def _():
        s_ref[...] = s0_ref[...]
        run(True)


def _hgrn_scan(u, lb_logits, layer, T=512, HP=4, BB=2):
    B, L, _ = u.shape
    n = L // T
    w = HP * HEAD_DIM
    qf, qb = _bidir_specs(T, n, HEAD_DIM, P_HG_Q, P_HG_Q, HP, BB)
    ff, fb = _bidir_specs(T, n, HEAD_DIM, P_HG_FF, P_HG_FB, HP, BB)
    vf, vb = _bidir_specs(T, n, HEAD_DIM, P_HG_I, P_HG_I, HP, BB)
    of, ob = _bidir_specs(T, n, HEAD_DIM, 0, 0, HP, BB)
    oshape = jax.ShapeDtypeStruct((B, L, GROUP_WIDTH), F32)
    return pl.pallas_call(
        functools.partial(_hgrn_kernel, layer=layer, T=T, HP=HP, BB=BB),
        grid=(B // BB, N_HEADS // HP, n),
        in_specs=[qf, qb, ff, fb, vf, vb,
                  pl.BlockSpec((DEPTH, 2, 1, w), lambda b, hg, t: (0, 0, 0, hg))],
        out_specs=[of, ob],
        out_shape=[oshape, oshape],
        scratch_shapes=[pltpu.VMEM((2 * HP * BB, HEAD_DIM, HEAD_DIM), F32),
                        pltpu.VMEM((2 * HP * BB, HEAD_DIM, HEAD_DIM), F32),
                        pltpu.VMEM((CHUNK, HEAD_DIM), F32),
                        pltpu.VMEM((CHUNK, HEAD_DIM), F32)],
        compiler_params=_cparams("parallel", "parallel", "arbitrary"),
        name="hgrn_scan",
    )(u, u, u, u, u, u, lb_logits.reshape(DEPTH, 2, 1, GROUP_WIDTH))


DN_HALO = 8
DN_GATE_GROUP = 4


def _dn_prep_kernel(*refs, tl):
    w_ref = refs[9]
    outs = refs[10:13]
    t = pl.program_id(1)
    has_prev = t > 0
    has_next = t < pl.num_programs(1) - 1
    ext_rows = tl + 2 * DN_HALO
    for kind in range(3):
        prev_ref, cur_ref, next_ref = refs[3 * kind:3 * kind + 3]
        for h in range(N_HEADS):
            cols = slice(h * HEAD_DIM, (h + 1) * HEAD_DIM)
            wcols = slice(kind * GROUP_WIDTH + h * HEAD_DIM, kind * GROUP_WIDTH + (h + 1) * HEAD_DIM)
            ext = jnp.concatenate([
                jnp.where(has_prev, prev_ref[0, :, cols], 0.0),
                cur_ref[0, :, cols],
                jnp.where(has_next, next_ref[0, :, cols], 0.0)], axis=0)
            acc = None
            for j in range(DN_CONV):
                tap = pltpu.roll(ext, (DN_CONV // 2 - j) % ext_rows, 0)[DN_HALO:DN_HALO + tl]
                term = tap * w_ref[j:j + 1, wcols]
                acc = term if acc is None else acc + term
            y = _silu(acc)
            if kind < 2:
                y = y * lax.rsqrt(jnp.sum(y * y, axis=-1, keepdims=True) + NORM_EPS)
            if kind == 0:
                y = y * ATTN_SCALE
            outs[kind][0, :, cols] = y


def _dn_prep(u, conv_w, tl=512):
    B, L, _ = u.shape
    nh = tl // DN_HALO
    last = L // DN_HALO - 1
    in_specs = []
    for kind in range(3):
        part = P_DN_QKV + kind
        in_specs += [
            pl.BlockSpec((1, DN_HALO, GROUP_WIDTH),
                         lambda b, t, part=part: (b, jnp.maximum(t * nh - 1, 0), part)),
            pl.BlockSpec((1, tl, GROUP_WIDTH), lambda b, t, part=part: (b, t, part)),
            pl.BlockSpec((1, DN_HALO, GROUP_WIDTH),
                         lambda b, t, part=part: (b, jnp.minimum((t + 1) * nh, last), part)),
        ]
    in_specs.append(pl.BlockSpec((DN_CONV, 3 * GROUP_WIDTH), lambda b, t: (0, 0)))
    ospec = pl.BlockSpec((1, tl, GROUP_WIDTH), lambda b, t: (b, t, 0))
    return pl.pallas_call(
        functools.partial(_dn_prep_kernel, tl=tl),
        grid=(B, L // tl),
        in_specs=in_specs,
        out_specs=[ospec, ospec, ospec],
        out_shape=[jax.ShapeDtypeStruct((B, L, GROUP_WIDTH), F32)] * 3,
        compiler_params=_cparams("parallel", "parallel"),
        name="dn_prep",
    )(*([u] * 9), conv_w)


def _softplus(x):
    return jnp.maximum(x, 0.0) + jnp.log1p(jnp.exp(-jnp.abs(x)))


def _dn_gates_kernel(ab_ref, alog_ref, dtb_ref, cols_ref, rows_ref, *, T):
    lane = _iota2((1, HEAD_DIM), 1)
    neg_a = -jnp.exp(alog_ref[...])
    tri_f, tri_b, eye4 = _tri4(True), _tri4(False), _eye4()

    def body(it, carry):
        st = []
        for j in range(DN_GATE_GROUP):
            c = it * DN_GATE_GROUP + j
            rows = pl.ds(pl.multiple_of(c * CHUNK, CHUNK), CHUNK)
            ab = ab_ref[0, rows, :]
            st.append(dict(c=c, rows=rows, ab=ab, g3=_stack3(neg_a * _softplus(ab + dtb_ref[...]))))
        for s in st:
            s["G"] = jnp.where(lane < N_HEADS, _dot(tri_f, s["g3"]), _dot(tri_b, s["g3"]))
        for s in st:
            cols_ref[0, s["rows"], :] = jnp.where(lane < 2 * N_HEADS, s["G"], _sigmoid(s["ab"]))
            rows_ref[0, s["c"]] = _dot_tn(_stack3(s["G"]), eye4)[0:4 * N_HEADS]
        return carry
    lax.fori_loop(0, T // CHUNK // DN_GATE_GROUP, body, 0)


def _dn_gates(uab, a_log, dt_bias, T=1024):
    B, L, _ = uab.shape
    pad = HEAD_DIM - 2 * N_HEADS
    alog = jnp.pad(a_log.astype(F32).reshape(1, 2 * N_HEADS), ((0, 0), (0, pad)))
    dtb = jnp.pad(dt_bias.astype(F32).reshape(1, 2 * N_HEADS), ((0, 0), (0, pad)))
    pspec = pl.BlockSpec((1, HEAD_DIM), lambda b, t: (0, 0))
    return pl.pallas_call(
        functools.partial(_dn_gates_kernel, T=T),
        grid=(B, L // T),
        in_specs=[pl.BlockSpec((1, T, HEAD_DIM), lambda b, t: (b, t, 0)), pspec, pspec],
        out_specs=[pl.BlockSpec((1, T, HEAD_DIM), lambda b, t: (b, t, 0)),
                   pl.BlockSpec((1, T // CHUNK, 4 * N_HEADS, HEAD_DIM), lambda b, t: (b, t, 0, 0))],
        out_shape=[jax.ShapeDtypeStruct((B, L, HEAD_DIM), F32),
                   jax.ShapeDtypeStruct((B, L // CHUNK, 4 * N_HEADS, HEAD_DIM), F32)],
        compiler_params=_cparams("parallel", "parallel"),
        name="dn_gates",
    )(uab, alog, dtb)


def _lhs3(x):
    hi, lo = _split2(x)
    return jnp.concatenate([hi, lo], axis=1)


def _rhs3(x):
    hi, lo = _split2(x)
    return jnp.concatenate([hi, lo, hi, jnp.zeros_like(hi)], axis=0)


def _dn_kernel(qf_ref, qb_ref, kf_ref, kb_ref, vf_ref, vb_ref, gcf_ref, gcb_ref, grf_ref, grb_ref,
               of_ref, ob_ref, s_ref, *, T, HP, BB):
    hg = pl.program_id(1)

    @pl.when(pl.program_id(2) == 0)
    def _():
        s_ref[...] = jnp.zeros_like(s_ref)

    n_chunks = T // CHUNK
    lane = _iota2((1, HEAD_DIM), 1)
    eye = jnp.where(_iota2((CHUNK, HEAD_DIM), 0) == (_iota2((CHUNK, HEAD_DIM), 1) & (CHUNK - 1)), 1.0, 0.0)

    chains = []
    for bi in range(BB):
        for hh in range(HP):
            for d in range(2):
                fwd = d == 0
                incl, strict = _order_masks(fwd, HEAD_DIM)
                chains.append(dict(
                    idx=(bi * HP + hh) * 2 + d, bi=bi, cols=slice(hh * HEAD_DIM, (hh + 1) * HEAD_DIM), fwd=fwd,
                    col=d * N_HEADS + hg * HP + hh, incl=incl, strict=strict,
                    q=(qf_ref, qb_ref)[d], k=(kf_ref, kb_ref)[d], v=(vf_ref, vb_ref)[d],
                    gc=(gcf_ref, gcb_ref)[d], gr=(grf_ref, grb_ref)[d], o=(of_ref, ob_ref)[d]))

    def body(step, carry):
        st = []
        for ch in chains:
            cols, col, bi = ch["cols"], ch["col"], ch["bi"]
            c = step if ch["fwd"] else n_chunks - 1 - step
            rows = pl.ds(pl.multiple_of(c * CHUNK, CHUNK), CHUNK)
            gates = ch["gc"][bi, rows, :]
            k = ch["k"][bi, rows, cols]
            kh = k.astype(BF16)
            st.append(dict(
                rows=rows, q=ch["q"][bi, rows, cols], k=k, v=ch["v"][bi, rows, cols],
                k2=jnp.concatenate([kh, kh], axis=0),
                g_col=jnp.sum(jnp.where(lane == col, gates, 0.0), axis=-1, keepdims=True),
                beta=jnp.sum(jnp.where(lane == col + 2 * N_HEADS, gates, 0.0), axis=-1, keepdims=True),
                g_row=ch["gr"][bi, c, pl.ds(col, 1), :]))
        for ch, s in zip(chains, st):
            s["gamma"] = jnp.where(ch["incl"], jnp.exp(jnp.minimum(s["g_col"] - s["g_row"], 0.0)), 0.0)
            s["kb"] = s["k"] * s["beta"]
        for ch, s in zip(chains, st):
            n_mat = jnp.where(ch["strict"], _dot_nt(s["kb"].astype(BF16), s["k2"]) * s["gamma"], 0.0)
            s["x"] = eye - n_mat
            s["nl"], s["nr"] = _lhs3(n_mat), _rhs3(n_mat)

        order = 1
        while True:
            for s in st:
                s["nk"] = _dot(s["nl"], s["nr"])
            for s in st:
                s["nr"] = _rhs3(s["nk"])
                s["x"] = s["x"] + _dot(_lhs3(s["x"]), s["nr"])
            order *= 2
            if 2 * order >= CHUNK:
                break
            for s in st:
                s["nl"] = _lhs3(s["nk"])

        for s in st:
            s["eg"] = jnp.exp(s["g_col"])
            rhs = jnp.concatenate([s["kb"] * s["eg"], s["v"] * s["beta"]], axis=1)
            s["wu"] = _dot(_lhs3(s["x"]), _rhs3(rhs))
        for ch, s in zip(chains, st):
            s["a_qk"] = (_dot_nt(s["q"].astype(BF16), s["k2"]) * s["gamma"])[:, 0:CHUNK].astype(BF16)
            s["s"] = s_ref[ch["idx"]]
            s["sb"] = s["s"].astype(BF16)
        for s in st:
            s["v_new"] = (s["wu"][:, HEAD_DIM:] - _dot(s["wu"][:, :HEAD_DIM].astype(BF16), s["sb"])).astype(BF16)
        for ch, s in zip(chains, st):
            ch["o"][ch["bi"], s["rows"], ch["cols"]] = (_dot((s["q"] * s["eg"]).astype(BF16), s["sb"])
                                                 + _dot(s["a_qk"], s["v_new"]))
        for ch, s in zip(chains, st):
            g_last = s["g_col"][CHUNK - 1:CHUNK] if ch["fwd"] else s["g_col"][0:1]
            k_dec = (s["k"] * jnp.exp(g_last - s["g_col"])).astype(BF16)
            s_ref[ch["idx"]] = jnp.exp(g_last) * s["s"] + _dot_tn(k_dec, s["v_new"])
        return carry
    lax.fori_loop(0, n_chunks, body, 0)


def _dn_scan(qh, kh, vh, gate_cols, gate_rows, T=512, HP=4, BB=2):
    B, L, _ = qh.shape
    n = L // T
    hf, hb = _bidir_specs(T, n, HEAD_DIM, 0, 0, HP, BB)
    gcf = pl.BlockSpec((BB, T, HEAD_DIM), lambda b, hg, t: (b, t, 0))
    gcb = pl.BlockSpec((BB, T, HEAD_DIM), lambda b, hg, t: (b, n - 1 - t, 0))
    rshape = (BB, T // CHUNK, 4 * N_HEADS, HEAD_DIM)
    grf = pl.BlockSpec(rshape, lambda b, hg, t: (b, t, 0, 0))
    grb = pl.BlockSpec(rshape, lambda b, hg, t: (b, n - 1 - t, 0, 0))
    oshape = jax.ShapeDtypeStruct((B, L, GROUP_WIDTH), F32)
    nc = 2 * HP * BB
    return pl.pallas_call(
        functools.partial(_dn_kernel, T=T, HP=HP, BB=BB),
        grid=(B // BB, N_HEADS // HP, n),
        in_specs=[hf, hb, hf, hb, hf, hb, gcf, gcb, grf, grb],
        out_specs=[hf, hb],
        out_shape=[oshape, oshape],
        scratch_shapes=[pltpu.VMEM((nc, HEAD_DIM, HEAD_DIM), F32)],
        compiler_params=_cparams("parallel", "parallel", "arbitrary"),
        name="dn_scan",
    )(qh, qh, kh, kh, vh, vh, gate_cols, gate_cols, gate_rows, gate_rows)


def _encoder_layer(x, mod, layer, p, rope_tabs):
    u, uab = _in_proj(x, p["norm1_g"][layer], mod, p["w_in_main"], p["w_in_ab"][layer], layer)

    swa_q, swa_k = _qk_prep(u, P_SWA_Q, P_SWA_K, None, p["swa_q_norm"][layer], p["swa_k_norm"][layer],
                            rope_tabs, F32)
    y_swa = _swa_attention(swa_q, swa_k, u)

    o_hg = _hgrn_scan(u, p["hgrn_lb_logits"], layer)

    na_q, na_k, na_v = _qk_prep(u, P_NA_Q, P_NA_K, P_NA_V, p["na_q_norm"][layer], p["na_k_norm"][layer],
                                None, BF16)
    y_na = _na_attention(na_q, na_k, na_v, p["na_bias"][layer])

    dq, dk, dv = _dn_prep(u, p["dn_conv_w"][layer])
    gate_cols, gate_rows = _dn_gates(uab, p["dn_a_log"][layer], p["dn_dt_bias"][layer])
    o_dn = _dn_scan(dq, dk, dv, gate_cols, gate_rows)

    x = _out_proj(x, mod, u, y_swa, y_na, o_hg, p["hgrn_norm_g"][layer], o_dn, p["dn_norm_g"][layer],
                  p["w_out"], layer)
    return _mlp(x, p["norm2_g"][layer], mod, p["w_mlp_in"], p["w_mlp_out"], layer)


def _run_trunk(x, mod_all, p):
    B, L, _ = x.shape
    rope_tabs = _rope_tables(L)
    for layer in range(DEPTH):
        x = _encoder_layer(x, mod_all[layer].reshape(B, 6, D_MODEL), layer, p, rope_tabs)
    return x


def kernel(x_prompt, x_sample, c_prompt, c_sample, norm1_g, norm2_g, ada_w, ada_b, w_in, w_out,
           swa_q_norm, swa_k_norm, hgrn_lb_logits, hgrn_norm_g, na_q_norm, na_k_norm, na_rpb,
           dn_conv_w, dn_a_log, dn_dt_bias, dn_norm_g, w_mlp_in, w_mlp_out):
    bp, bs = c_prompt.shape[0], c_sample.shape[0]
    pad_rows = -(bp + bs) % 8
    c_all = jnp.concatenate([c_prompt, c_sample, jnp.zeros((pad_rows, D_MODEL), F32)], axis=0)
    mod_all = _modulation(c_all, ada_w, ada_b)

    p = dict(
        norm1_g=norm1_g, norm2_g=norm2_g,
        w_in_main=w_in.astype(BF16),
        w_in_ab=jnp.pad(w_in[:, :, IN_MAIN:], ((0, 0), (0, 0), (0, HEAD_DIM - 4 * N_HEADS))).astype(BF16),
        w_out=w_out.astype(BF16),
        swa_q_norm=swa_q_norm, swa_k_norm=swa_k_norm,
        hgrn_lb_logits=hgrn_lb_logits.astype(F32), hgrn_norm_g=hgrn_norm_g,
        na_q_norm=na_q_norm, na_k_norm=na_k_norm,
        na_bias=jnp.stack([_na_bias_table(na_rpb[l]) for l in range(DEPTH)]),
        dn_conv_w=dn_conv_w, dn_a_log=dn_a_log, dn_dt_bias=dn_dt_bias, dn_norm_g=dn_norm_g,
        w_mlp_in=w_mlp_in.astype(BF16), w_mlp_out=w_mlp_out.astype(BF16),
    )
    y_prompt = _run_trunk(x_prompt, mod_all[:, :bp], p)
    y_sample = _run_trunk(x_sample, mod_all[:, bp:bp + bs], p)
    return (y_prompt, y_sample)
```
